```python
import jax, jax.numpy as jnp
from jax import lax
import numpy as np

D_MODEL = 1024
BATCH = 16
SEQ = 2048
DEPTH = 2

N_MIXERS = 2
HEAD_SIZE = 64
N_HEADS = D_MODEL // HEAD_SIZE
DECAY_LORA = max(32, int(round(D_MODEL ** 0.5 * 1.8 / 32)) * 32)
AAA_LORA = max(32, int(round(D_MODEL ** 0.5 * 1.8 / 32)) * 32)
GATE_LORA = max(32, int(round(D_MODEL ** 0.8 * 0.6 / 32)) * 32)
D_FF = int(round(8 * D_MODEL / 3 / 256)) * 256
CONV_WIDTH = 31
N_RWKV = (DEPTH + 1) // 2
N_CONV = DEPTH // 2
RMS_EPS = 1e-6
LN_EPS = 1e-5
GN_EPS = 64e-5
L2_EPS = 1e-12

kernel_name = 'hybrid_rwkv7_conformer_macaron'


def rms_norm(x, gain):
    x32 = x.astype(jnp.float32)
    y = x32 * lax.rsqrt(jnp.mean(x32 * x32, axis=-1, keepdims=True) + RMS_EPS)
    return (y * gain.astype(jnp.float32)).astype(x.dtype)


def swiglu(x, w_gate, w_up, w_down):
    return (jax.nn.silu(x @ w_gate) * (x @ w_up)) @ w_down


def rwkv7_time_mix(x, mu, w_rkv, w0, w1, w2, a0, a1, a2, g1, g2, k_k, k_a, r_k,
                   ln_gain, ln_bias, w_out):
    B, T, D = x.shape
    H, N = N_HEADS, HEAD_SIZE
    x_prev = jnp.pad(x[:, :-1], ((0, 0), (1, 0), (0, 0)))
    xx = x_prev - x
    xr = x + xx * mu[0]
    xw = x + xx * mu[1]
    xk = x + xx * mu[2]
    xv = x + xx * mu[3]
    xa = x + xx * mu[4]
    xg = x + xx * mu[5]

    r = xr @ w_rkv[0]
    k = xk @ w_rkv[1]
    v = xv @ w_rkv[2]
    w = -jax.nn.softplus(-(w0 + jnp.tanh(xw @ w1) @ w2)) - 0.5
    a = jax.nn.sigmoid(a0 + (xa @ a1) @ a2)
    g = jax.nn.sigmoid(xg @ g1) @ g2

    kk = (k * k_k).reshape(B, T, H, N).astype(jnp.float32)
    kk = kk / jnp.maximum(jnp.linalg.norm(kk, axis=-1, keepdims=True), L2_EPS)
    k = k * (1.0 + (a - 1.0) * k_a)

    f32 = jnp.float32
    r_h = r.reshape(B, T, H, N).astype(f32)
    k_h = k.reshape(B, T, H, N).astype(f32)
    v_h = v.reshape(B, T, H, N).astype(f32)
    decay = jnp.exp(-jnp.exp(w.reshape(B, T, H, N).astype(f32)))
    a_vec = -kk
    b_vec = kk * a.reshape(B, T, H, N).astype(f32)

    def step(S, inp):
        r_t, w_t, k_t, v_t, a_t, b_t = inp
        sa = jnp.einsum('bhij,bhj->bhi', S, a_t)
        S = (S * w_t[:, :, None, :] + sa[..., None] * b_t[:, :, None, :]
             + v_t[..., None] * k_t[:, :, None, :])
        y_t = jnp.einsum('bhij,bhj->bhi', S, r_t)
        return S, y_t

    to_time = lambda z: jnp.moveaxis(z, 1, 0)
    S0 = jnp.zeros((B, H, N, N), f32)
    _, ys = lax.scan(step, S0, (to_time(r_h), to_time(decay), to_time(k_h),
                                to_time(v_h), to_time(a_vec), to_time(b_vec)))
    y = jnp.moveaxis(ys, 0, 1)

    mean = jnp.mean(y, axis=-1, keepdims=True)
    var = jnp.mean(jnp.square(y - mean), axis=-1, keepdims=True)
    y = ((y - mean) * lax.rsqrt(var + GN_EPS)).reshape(B, T, D)
    y = y * ln_gain.astype(f32) + ln_bias.astype(f32)
    bonus = jnp.sum(r_h * k_h * r_k.astype(f32), axis=-1, keepdims=True) * v_h
    y = y + bonus.reshape(B, T, D)
    return ((y * g.astype(f32)) @ w_out.astype(f32)).astype(x.dtype)


def conformer_conv(x, w_in, b_in, dw, dw_b, ln_gain, ln_bias, w_out, b_out):
    D = x.shape[-1]
    h = x @ w_in + b_in
    h = h[..., :D] * jax.nn.sigmoid(h[..., D:])
    h = lax.conv_general_dilated(
        h, dw[:, None, :].astype(h.dtype), window_strides=(1,),
        padding=[(CONV_WIDTH - 1, 0)],
        dimension_numbers=('NWC', 'WIO', 'NWC'),
        feature_group_count=D) + dw_b
    h32 = h.astype(jnp.float32)
    mean = jnp.mean(h32, axis=-1, keepdims=True)
    var = jnp.mean(jnp.square(h32 - mean), axis=-1, keepdims=True)
    h32 = (h32 - mean) * lax.rsqrt(var + LN_EPS) * ln_gain.astype(jnp.float32) + ln_bias.astype(jnp.float32)
    h = jax.nn.silu(h32).astype(x.dtype)
    return h @ w_out + b_out


def setup_inputs(seed: int = 0) -> dict:
    key = jax.random.key(seed)
    ks = iter(jax.random.split(key, 64))
    nrm = lambda shape, scale: scale * jax.random.normal(next(ks), shape, jnp.float32)
    D, F = D_MODEL, D_FF

    ratio = (jnp.arange(D, dtype=jnp.float32) / (D - 1)) ** 0.85
    w0 = (-7.0 + 5.0 * ratio + 0.5)[None, :] + nrm((N_RWKV, D), 0.1)

    return {
        'x': nrm((BATCH, SEQ, D), 1.0),
        'norm_gains': 1.0 + nrm((DEPTH, 3, D), 0.02),
        'ffn_w_gate': nrm((DEPTH, 2, D, F), D ** -0.5),
        'ffn_w_up': nrm((DEPTH, 2, D, F), D ** -0.5),
        'ffn_w_down': nrm((DEPTH, 2, F, D), F ** -0.5),
        'rwkv_mu': jax.random.uniform(next(ks), (N_RWKV, 6, D), jnp.float32),
        'rwkv_w_rkv': nrm((N_RWKV, 3, D, D), D ** -0.5),
        'rwkv_w0': w0,
        'rwkv_w1': nrm((N_RWKV, D, DECAY_LORA), 0.1 * D ** -0.5),
        'rwkv_w2': nrm((N_RWKV, DECAY_LORA, D), 0.1 * DECAY_LORA ** -0.5),
        'rwkv_a0': nrm((N_RWKV, D), 0.1),
        'rwkv_a1': nrm((N_RWKV, D, AAA_LORA), 0.1 * D ** -0.5),
        'rwkv_a2': nrm((N_RWKV, AAA_LORA, D), 0.1 * AAA_LORA ** -0.5),
        'rwkv_g1': nrm((N_RWKV, D, GATE_LORA), D ** -0.5),
        'rwkv_g2': nrm((N_RWKV, GATE_LORA, D), GATE_LORA ** -0.5),
        'rwkv_k_k': 0.85 + nrm((N_RWKV, D), 0.05),
        'rwkv_k_a': 1.0 + nrm((N_RWKV, D), 0.05),
        'rwkv_r_k': -0.04 + nrm((N_RWKV, N_HEADS, HEAD_SIZE), 0.05),
        'rwkv_ln_gain': 1.0 + nrm((N_RWKV, D), 0.02),
        'rwkv_ln_bias': nrm((N_RWKV, D), 0.02),
        'rwkv_w_out': nrm((N_RWKV, D, D), D ** -0.5),
        'conv_w_in': nrm((N_CONV, D, 2 * D), D ** -0.5),
        'conv_b_in': nrm((N_CONV, 2 * D), 0.02),
        'conv_dw': nrm((N_CONV, CONV_WIDTH, D), CONV_WIDTH ** -0.5),
        'conv_dw_b': nrm((N_CONV, D), 0.02),
        'conv_ln_gain': 1.0 + nrm((N_CONV, D), 0.02),
        'conv_ln_bias': nrm((N_CONV, D), 0.02),
        'conv_w_out': nrm((N_CONV, D, D), D ** -0.5),
        'conv_b_out': nrm((N_CONV, D), 0.02),
        'final_norm': 1.0 + nrm((D,), 0.02),
    }


def reference(x, norm_gains, ffn_w_gate, ffn_w_up, ffn_w_down,
              rwkv_mu, rwkv_w_rkv, rwkv_w0, rwkv_w1, rwkv_w2, rwkv_a0, rwkv_a1, rwkv_a2,
              rwkv_g1, rwkv_g2, rwkv_k_k, rwkv_k_a, rwkv_r_k, rwkv_ln_gain, rwkv_ln_bias,
              rwkv_w_out, conv_w_in, conv_b_in, conv_dw, conv_dw_b, conv_ln_gain,
              conv_ln_bias, conv_w_out, conv_b_out, final_norm):
    for i in range(DEPTH):
        x = x + 0.5 * swiglu(rms_norm(x, norm_gains[i, 0]),
                             ffn_w_gate[i, 0], ffn_w_up[i, 0], ffn_w_down[i, 0])
        h = rms_norm(x, norm_gains[i, 1])
        j = i // N_MIXERS
        if i % N_MIXERS == 0:
            x = x + rwkv7_time_mix(h, rwkv_mu[j], rwkv_w_rkv[j], rwkv_w0[j], rwkv_w1[j],
                                   rwkv_w2[j], rwkv_a0[j], rwkv_a1[j], rwkv_a2[j],
                                   rwkv_g1[j], rwkv_g2[j], rwkv_k_k[j], rwkv_k_a[j],
                                   rwkv_r_k[j], rwkv_ln_gain[j], rwkv_ln_bias[j],
                                   rwkv_w_out[j])
        else:
            x = x + conformer_conv(h, conv_w_in[j], conv_b_in[j], conv_dw[j], conv_dw_b[j],
                                   conv_ln_gain[j], conv_ln_bias[j], conv_w_out[j],
                                   conv_b_out[j])
        x = x + 0.5 * swiglu(rms_norm(x, norm_gains[i, 2]),
                             ffn_w_gate[i, 1], ffn_w_up[i, 1], ffn_w_down[i, 1])
    return rms_norm(x, final_norm)
```

```python
import functools

import jax
import jax.numpy as jnp
from jax import lax
from jax.experimental import pallas as pl
from jax.experimental.pallas import tpu as pltpu

F32 = jnp.float32
BF16 = jnp.bfloat16

HEAD = 64
GROUP_LANES = 256
HEADS_PER_GROUP = GROUP_LANES // HEAD
CHUNK = 64
CONV_WIDTH = 31
CONV_HALO = 32
SUBLANES = 8
RMS_EPS = 1e-6
LN_EPS = 1e-5
GN_EPS = 64e-5
L2_EPS = 1e-12
DECAY_SCALE = 0.6065306597126334
VMEM_LIMIT = 56 * 1024 * 1024


def _rms(x, gain):
    ms = jnp.mean(x * x, axis=-1, keepdims=True)
    return x * lax.rsqrt(ms + RMS_EPS) * gain


def _dot(a, b):
    return jnp.dot(a, b, preferred_element_type=F32)


def _dot_nt(a, b):
    return lax.dot_general(a, b, (((1,), (1,)), ((), ())), preferred_element_type=F32)


def _dot_tn(a, b):
    return lax.dot_general(a, b, (((0,), (0,)), ((), ())), preferred_element_type=F32)


def _split2(x):
    hi = x.astype(BF16)
    lo = (x - hi.astype(F32)).astype(BF16)
    return hi, lo


def _split3(x):
    hi = x.astype(BF16)
    r1 = x - hi.astype(F32)
    mid = r1.astype(BF16)
    lo = (r1 - mid.astype(F32)).astype(BF16)
    return hi, mid, lo


def _head_sum(x, ones_bd):
    hi, lo = _split2(x)
    outs = []
    for s in range(x.shape[-1] // GROUP_LANES):
        sl = slice(s * GROUP_LANES, (s + 1) * GROUP_LANES)
        outs.append(_dot(hi[:, sl], ones_bd) + _dot(lo[:, sl], ones_bd))
    return jnp.concatenate(outs, axis=-1)


def _ffn_kernel(x_ref, g_ref, wg_ref, wu_ref, wd_ref, o_ref):
    x = x_ref[...]
    h = _rms(x, g_ref[...]).astype(BF16)
    gate = _dot(h, wg_ref[...])
    up = _dot(h, wu_ref[...])
    act = (gate * jax.nn.sigmoid(gate) * up).astype(BF16)
    o_ref[...] = x + 0.5 * _dot(act, wd_ref[...])


def _const_spec(shape):
    nd = len(shape)
    return pl.BlockSpec(shape, lambda *_: (0,) * nd, pipeline_mode=pl.Buffered(1))


def _ffn(x2, gain, wg, wu, wd, tm):
    m, d = x2.shape
    f = wg.shape[1]
    return pl.pallas_call(
        _ffn_kernel,
        grid=(m // tm,),
        in_specs=[pl.BlockSpec((tm, d), lambda i: (i, 0)),
                  _const_spec((1, d)), _const_spec((d, f)), _const_spec((d, f)), _const_spec((f, d))],
        out_specs=pl.BlockSpec((tm, d), lambda i: (i, 0)),
        out_shape=jax.ShapeDtypeStruct((m, d), F32),
        compiler_params=pltpu.CompilerParams(dimension_semantics=("arbitrary",),
                                             vmem_limit_bytes=VMEM_LIMIT),
        name="ffn",
    )(x2, gain, wg, wu, wd)


def _rwkv_proj_kernel(x_ref, gain_ref, mu_ref, wr_ref, wk_ref, wv_ref, w0_ref, w1_ref, w2_ref,
                      a0_ref, a1_ref, a2_ref, g1_ref, g2_ref, kk_ref, ka_ref, ones_ref,
                      r_out, lw_out, k_out, v_out, a_out, b_out, g_out, carry_ref):
    t = pl.program_id(1)

    @pl.when(t == 0)
    def _():
        carry_ref[...] = jnp.zeros_like(carry_ref)

    h = _rms(x_ref[0], gain_ref[...])
    tm = h.shape[0]
    row = lax.broadcasted_iota(jnp.int32, h.shape, 0)
    h_prev = jnp.where(row == 0, carry_ref[0:1, :], pltpu.roll(h, 1, axis=0))
    carry_ref[0:1, :] = h[tm - 1:tm, :]
    xx = h_prev - h
    mix = lambda i: (h + xx * mu_ref[i:i + 1, :]).astype(BF16)

    r = _dot(mix(0), wr_ref[...])
    k = _dot(mix(2), wk_ref[...])
    v = _dot(mix(3), wv_ref[...])
    zw = w0_ref[...] + _dot(jnp.tanh(_dot(mix(1), w1_ref[...])).astype(BF16), w2_ref[...])
    lw = -DECAY_SCALE * jax.nn.sigmoid(zw)
    a = jax.nn.sigmoid(a0_ref[...] + _dot(_dot(mix(4), a1_ref[...]).astype(BF16), a2_ref[...]))
    g = _dot(jax.nn.sigmoid(_dot(mix(5), g1_ref[...])).astype(BF16), g2_ref[...])

    kk = k * kk_ref[...]
    norm = jnp.sqrt(_head_sum(kk * kk, ones_ref[...]))
    kk = kk / jnp.maximum(norm, L2_EPS)

    r_out[0] = r
    lw_out[0] = lw
    k_out[0] = k * (1.0 + (a - 1.0) * ka_ref[...])
    v_out[0] = v
    a_out[0] = -kk
    b_out[0] = kk * a
    g_out[0] = g


def _rwkv_proj(x, gain, mu, wr, wk, wv, w0, w1, w2, a0, a1, a2, g1, g2, k_k, k_a, ones_bd, tm):
    b, t, d = x.shape
    tok = pl.BlockSpec((1, tm, d), lambda i, j: (i, j, 0))
    consts = [gain, mu, wr, wk, wv, w0, w1, w2, a0, a1, a2, g1, g2, k_k, k_a, ones_bd]
    return pl.pallas_call(
        _rwkv_proj_kernel,
        grid=(b, t // tm),
        in_specs=[tok] + [_const_spec(c.shape) for c in consts],
        out_specs=[tok] * 7,
        out_shape=[jax.ShapeDtypeStruct((b, t, d), F32)] * 7,
        scratch_shapes=[pltpu.VMEM((SUBLANES, d), F32)],
        compiler_params=pltpu.CompilerParams(dimension_semantics=("arbitrary", "arbitrary"),
                                             vmem_limit_bytes=VMEM_LIMIT),
        name="rwkv_proj",
    )(x, *consts)


def _stack_heads(x, blk):
    return jnp.concatenate(
        [jnp.where(blk == h, x, 0.0).astype(BF16) for h in range(HEADS_PER_GROUP)], axis=0)


def _scan_group(r_t, a_t, k_t, b_t, k_h, b_h, v, g_end, s0):
    c = r_t.shape[0]
    assert HEADS_PER_GROUP * c == GROUP_LANES
    rowi = lax.broadcasted_iota(jnp.int32, (c, GROUP_LANES), 0)
    lane = lax.broadcasted_iota(jnp.int32, (c, GROUP_LANES), 1)
    blk = lane // HEAD
    col = lane - blk * HEAD
    strict = col < rowi
    incl = col <= rowi

    lhs = jnp.concatenate([a_t, r_t], axis=0).astype(BF16)
    rhs = jnp.concatenate([_stack_heads(b_t, blk), _stack_heads(k_t, blk)], axis=0)
    aa = _dot_nt(lhs, rhs)
    a_ab = jnp.where(strict, aa[:c, :GROUP_LANES], 0.0)
    a_ak = jnp.where(strict, aa[:c, GROUP_LANES:], 0.0)
    a_rb = jnp.where(incl, aa[c:, :GROUP_LANES], 0.0)
    a_rk = jnp.where(incl, aa[c:, GROUP_LANES:], 0.0)

    p = a_ab
    tinv = jnp.where(col == rowi, 1.0, 0.0) + p
    n_sq = c.bit_length() - 1
    p = _dot(p.astype(BF16), _stack_heads(p, blk))
    for i in range(1, n_sq):
        bd = _stack_heads(p, blk)
        if i < n_sq - 1:
            both = _dot(jnp.concatenate([p, tinv], axis=0).astype(BF16), bd)
            p = both[:c]
            tinv = tinv + both[c:]
        else:
            tinv = tinv + _dot(tinv.astype(BF16), bd)

    v_bd = _stack_heads(v, blk)
    akv = _dot(a_ak.astype(BF16), v_bd)
    tx = _dot(tinv.astype(BF16),
              jnp.concatenate([_stack_heads(a_t, blk), _stack_heads(akv, blk)], axis=1))
    a_hat = tx[:, :GROUP_LANES]
    u0 = tx[:, GROUP_LANES:]
    ry = _dot(a_rb.astype(BF16),
              jnp.concatenate([_stack_heads(a_hat, blk), _stack_heads(u0, blk)], axis=1))
    r_hat = r_t + ry[:, :GROUP_LANES]
    y0 = ry[:, GROUP_LANES:] + _dot(a_rk.astype(BF16), v_bd)

    s0b = s0.astype(BF16)
    y = _dot_nt(r_hat.astype(BF16), s0b) + y0
    gmat = _dot_tn(a_hat.astype(BF16), b_h.astype(BF16))
    hmat = _dot_tn(jnp.concatenate([u0, v], axis=0).astype(BF16),
                   jnp.concatenate([b_h, k_h], axis=0).astype(BF16))
    srow = lax.broadcasted_iota(jnp.int32, (GROUP_LANES, GROUP_LANES), 0) // HEAD
    scol = lax.broadcasted_iota(jnp.int32, (GROUP_LANES, GROUP_LANES), 1) // HEAD
    s_new = jnp.where(srow == scol, s0 * g_end + _dot(s0b, gmat.astype(BF16)) + hmat, 0.0)
    return y, s_new


def _scan_chunk(r, lw, k, v, a, b, tri, states):
    c, d = r.shape
    hi, mid, lo = _split3(lw)
    cs = _dot(tri, hi) + _dot(tri, mid) + _dot(tri, lo)
    cs_end = cs[c - 1:c, :]
    e_pos = jnp.exp(cs)
    e_neg = jnp.exp(-cs)
    e_end = jnp.exp(cs_end - cs)
    r_t = r * e_pos
    a_t = a * jnp.exp(cs - lw)
    k_t = k * e_neg
    b_t = b * e_neg
    k_h = k * e_end
    b_h = b * e_end
    g_end = jnp.exp(cs_end)
    ys, new_states = [], []
    for g in range(d // GROUP_LANES):
        sl = slice(g * GROUP_LANES, (g + 1) * GROUP_LANES)
        y, s_new = _scan_group(r_t[:, sl], a_t[:, sl], k_t[:, sl], b_t[:, sl], k_h[:, sl], b_h[:, sl],
                               v[:, sl], g_end[:, sl], states[g])
        ys.append(y)
        new_states.append(s_new)
    return jnp.concatenate(ys, axis=-1), new_states


def _rwkv_scan_kernel(r_ref, lw_ref, k_ref, v_ref, a_ref, b_ref, tri_ref, y_ref, s_ref):
    @pl.when(pl.program_id(1) == 0)
    def _():
        s_ref[...] = jnp.zeros_like(s_ref)

    n_groups = s_ref.shape[0]
    y, new_states = _scan_chunk(r_ref[0], lw_ref[0], k_ref[0], v_ref[0], a_ref[0], b_ref[0], tri_ref[...],
                                [s_ref[g] for g in range(n_groups)])
    y_ref[0] = y
    for g in range(n_groups):
        s_ref[g] = new_states[g]


def _rwkv_scan(r, lw, k, v, a, b, tri):
    bsz, t, d = r.shape
    tok = pl.BlockSpec((1, CHUNK, d), lambda i, j: (i, j, 0))
    return pl.pallas_call(
        _rwkv_scan_kernel,
        grid=(bsz, t // CHUNK),
        in_specs=[tok] * 6 + [_const_spec(tri.shape)],
        out_specs=tok,
        out_shape=jax.ShapeDtypeStruct((bsz, t, d), F32),
        scratch_shapes=[pltpu.VMEM((d // GROUP_LANES, GROUP_LANES, GROUP_LANES), F32)],
        compiler_params=pltpu.CompilerParams(dimension_semantics=("arbitrary", "arbitrary"),
                                             vmem_limit_bytes=VMEM_LIMIT),
        name="rwkv_scan",
    )(r, lw, k, v, a, b, tri)


def _rwkv_out_kernel(x_ref, y_ref, r_ref, k_ref, v_ref, g_ref, rk_ref, lng_ref, lnb_ref, wo_ref, ones_ref,
                     o_ref):
    ones_bd = ones_ref[...]
    y = y_ref[...]
    mean = _head_sum(y, ones_bd) * (1.0 / HEAD)
    dlt = y - mean
    var = _head_sum(dlt * dlt, ones_bd) * (1.0 / HEAD)
    yn = dlt * lax.rsqrt(var + GN_EPS) * lng_ref[...] + lnb_ref[...]
    bonus = _head_sum(r_ref[...] * k_ref[...] * rk_ref[...], ones_bd) * v_ref[...]
    out = ((yn + bonus) * g_ref[...]).astype(BF16)
    o_ref[...] = x_ref[...] + _dot(out, wo_ref[...])


def _rwkv_out(x2, y2, r2, k2, v2, g2, r_k, ln_g, ln_b, wo, ones_bd, tm):
    m, d = x2.shape
    tok = pl.BlockSpec((tm, d), lambda i: (i, 0))
    consts = [r_k, ln_g, ln_b, wo, ones_bd]
    return pl.pallas_call(
        _rwkv_out_kernel,
        grid=(m // tm,),
        in_specs=[tok] * 6 + [_const_spec(c.shape) for c in consts],
        out_specs=tok,
        out_shape=jax.ShapeDtypeStruct((m, d), F32),
        compiler_params=pltpu.CompilerParams(dimension_semantics=("arbitrary",),
                                             vmem_limit_bytes=VMEM_LIMIT),
        name="rwkv_out",
    )(x2, y2, r2, k2, v2, g2, *consts)


CONV_ROWS = 32
CONV_LANES = 512


def _conv_kernel(x_ref, gain_ref, win_ref, bin_ref, dw_ref, dwb_ref, lng_ref, lnb_ref, wout_ref, bout_ref,
                 o_ref, ext_ref, sh_ref, cv_ref):
    tm, d = o_ref.shape[1], o_ref.shape[2]

    @pl.when(pl.program_id(1) == 0)
    def _():
        ext_ref[0:CONV_HALO, :] = jnp.zeros((CONV_HALO, d), F32)

    x = x_ref[0]
    h = _rms(x, gain_ref[...]).astype(BF16)
    z = _dot(h, win_ref[...]) + bin_ref[...]
    ext_ref[CONV_HALO:CONV_HALO + tm, :] = z[:, :d] * jax.nn.sigmoid(z[:, d:])

    off = CONV_HALO - (CONV_WIDTH - 1)
    sh_rows = sh_ref.shape[1]
    for s in range(1, SUBLANES):
        sh_ref[s] = ext_ref[s:s + sh_rows, :]

    def row_block(rb, carry):
        base = pl.multiple_of(rb * CONV_ROWS, CONV_ROWS)
        lanes = min(CONV_LANES, d)
        for lb in range(d // lanes):
            ls = slice(lb * lanes, (lb + 1) * lanes)
            acc = jnp.broadcast_to(dwb_ref[:, ls], (CONV_ROWS, lanes))
            for j in range(CONV_WIDTH):
                q, s = divmod(off + j, SUBLANES)
                rows = pl.ds(base + q * SUBLANES, CONV_ROWS)
                win = ext_ref[rows, ls] if s == 0 else sh_ref[s, rows, ls]
                acc = acc + dw_ref[j:j + 1, ls] * win
            cv_ref[pl.ds(base, CONV_ROWS), ls] = acc
        return carry

    lax.fori_loop(0, tm // CONV_ROWS, row_block, 0)
    ext_ref[0:CONV_HALO, :] = ext_ref[tm:tm + CONV_HALO, :]

    cv = cv_ref[...]
    mean = jnp.mean(cv, axis=-1, keepdims=True)
    dlt = cv - mean
    var = jnp.mean(dlt * dlt, axis=-1, keepdims=True)
    hn = dlt * lax.rsqrt(var + LN_EPS) * lng_ref[...] + lnb_ref[...]
    act = (hn * jax.nn.sigmoid(hn)).astype(BF16)
    o_ref[0] = x + _dot(act, wout_ref[...]) + bout_ref[...]


def _conv(x, gain, w_in, b_in, dw, dw_b, ln_g, ln_b, w_out, b_out, tm):
    b, t, d = x.shape
    tok = pl.BlockSpec((1, tm, d), lambda i, j: (i, j, 0))
    consts = [gain, w_in, b_in, dw, dw_b, ln_g, ln_b, w_out, b_out]
    sh_rows = tm + CONV_HALO - SUBLANES
    return pl.pallas_call(
        _conv_kernel,
        grid=(b, t // tm),
        in_specs=[tok] + [_const_spec(c.shape) for c in consts],
        out_specs=tok,
        out_shape=jax.ShapeDtypeStruct((b, t, d), F32),
        scratch_shapes=[pltpu.VMEM((tm + CONV_HALO, d), F32),
                        pltpu.VMEM((SUBLANES, sh_rows, d), F32),
                        pltpu.VMEM((tm, d), F32)],
        compiler_params=pltpu.CompilerParams(dimension_semantics=("arbitrary", "arbitrary"),
                                             vmem_limit_bytes=VMEM_LIMIT),
        name="conformer_conv",
    )(x, *consts)


def _final_kernel(x_ref, g_ref, o_ref):
    o_ref[...] = _rms(x_ref[...], g_ref[...])


def _final_norm(x2, gain, tm):
    m, d = x2.shape
    return pl.pallas_call(
        _final_kernel,
        grid=(m // tm,),
        in_specs=[pl.BlockSpec((tm, d), lambda i: (i, 0)), _const_spec((1, d))],
        out_specs=pl.BlockSpec((tm, d), lambda i: (i, 0)),
        out_shape=jax.ShapeDtypeStruct((m, d), F32),
        compiler_params=pltpu.CompilerParams(dimension_semantics=("arbitrary",)),
        name="final_norm",
    )(x2, gain)


def _row(v):
    return v.reshape(1, -1)


def kernel(x, norm_gains, ffn_w_gate, ffn_w_up, ffn_w_down, rwkv_mu, rwkv_w_rkv, rwkv_w0, rwkv_w1, rwkv_w2, rwkv_a0, rwkv_a1, rwkv_a2, rwkv_g1, rwkv_g2, rwkv_k_k, rwkv_k_a, rwkv_r_k, rwkv_ln_gain, rwkv_ln_bias, rwkv_w_out, conv_w_in, conv_b_in, conv_dw, conv_dw_b, conv_ln_gain, conv_ln_bias, conv_w_out, conv_b_out, final_norm):
    bsz, t, d = x.shape
    depth = norm_gains.shape[0]
    m = bsz * t
    tm_ffn = min(512, m)
    tm_tok = min(256, t)
    assert d % GROUP_LANES == 0 and t % CHUNK == 0 and t % tm_tok == 0 and m % tm_ffn == 0

    bf = lambda w: w.astype(BF16)
    blk = jnp.arange(GROUP_LANES, dtype=jnp.int32) // HEAD
    ones_bd = (blk[:, None] == blk[None, :]).astype(BF16)
    idx = jnp.arange(CHUNK, dtype=jnp.int32)
    tri = (idx[:, None] >= idx[None, :]).astype(BF16)

    def ffn(xc, i, s):
        out = _ffn(xc.reshape(m, d), _row(norm_gains[i, 2 * s]), bf(ffn_w_gate[i, s]), bf(ffn_w_up[i, s]),
                   bf(ffn_w_down[i, s]), tm_ffn)
        return out.reshape(bsz, t, d)

    for i in range(depth):
        x = ffn(x, i, 0)
        j = i // 2
        gain = _row(norm_gains[i, 1])
        if i % 2 == 0:
            r, lw, k, v, a, b, g = _rwkv_proj(
                x, gain, rwkv_mu[j], bf(rwkv_w_rkv[j, 0]), bf(rwkv_w_rkv[j, 1]), bf(rwkv_w_rkv[j, 2]),
                _row(rwkv_w0[j]), bf(rwkv_w1[j]), bf(rwkv_w2[j]), _row(rwkv_a0[j]), bf(rwkv_a1[j]),
                bf(rwkv_a2[j]), bf(rwkv_g1[j]), bf(rwkv_g2[j]), _row(rwkv_k_k[j]), _row(rwkv_k_a[j]),
                ones_bd, tm_tok)
            y = _rwkv_scan(r, lw, k, v, a, b, tri)
            f2 = lambda z: z.reshape(m, d)
            x = _rwkv_out(f2(x), f2(y), f2(r), f2(k), f2(v), f2(g), _row(rwkv_r_k[j]), _row(rwkv_ln_gain[j]),
                          _row(rwkv_ln_bias[j]), bf(rwkv_w_out[j]), ones_bd, tm_ffn).reshape(bsz, t, d)
        else:
            x = _conv(x, gain, bf(conv_w_in[j]), _row(conv_b_in[j]), conv_dw[j], _row(conv_dw_b[j]),
                      _row(conv_ln_gain[j]), _row(conv_ln_bias[j]), bf(conv_w_out[j]), _row(conv_b_out[j]),
                      tm_tok)
        x = ffn(x, i, 1)
    return _final_norm(x.reshape(m, d), _row(final_norm), tm_ffn).reshape(bsz, t, d)
```

```python
import functools

import jax
import jax.numpy as jnp
from jax import lax
from jax.experimental import pallas as pl
from jax.experimental.pallas import tpu as pltpu

F32 = jnp.float32
BF16 = jnp.bfloat16

HEAD = 64
GROUP_LANES = 256
HEADS_PER_GROUP = GROUP_LANES // HEAD
CHUNK = 64
CONV_WIDTH = 31
CONV_HALO = 32
SUBLANES = 8
RMS_EPS = 1e-6
LN_EPS = 1e-5
GN_EPS = 64e-5
L2_EPS = 1e-12
DECAY_SCALE = 0.6065306597126334
VMEM_LIMIT = 56 * 1024 * 1024


def _rms(x, gain):
    ms = jnp.mean(x * x, axis=-1, keepdims=True)
    return x * lax.rsqrt(ms + RMS_EPS) * gain


def _dot(a, b):
    return jnp.dot(a, b, preferred_element_type=F32)


def _dot_nt(a, b):
    return lax.dot_general(a, b, (((1,), (1,)), ((), ())), preferred_element_type=F32)


def _dot_tn(a, b):
    return lax.dot_general(a, b, (((0,), (0,)), ((), ())), preferred_element_type=F32)


def _split2(x):
    hi = x.astype(BF16)
    lo = (x - hi.astype(F32)).astype(BF16)
    return hi, lo


def _split3(x):
    hi = x.astype(BF16)
    r1 = x - hi.astype(F32)
    mid = r1.astype(BF16)
    lo = (r1 - mid.astype(F32)).astype(BF16)
    return hi, mid, lo


def _head_sum(x, ones_bd):
    hi, lo = _split2(x)
    outs = []
    for s in range(x.shape[-1] // GROUP_LANES):
        sl = slice(s * GROUP_LANES, (s + 1) * GROUP_LANES)
        outs.append(_dot(hi[:, sl], ones_bd) + _dot(lo[:, sl], ones_bd))
    return jnp.concatenate(outs, axis=-1)


def _ffn_kernel(x_ref, g_ref, wg_ref, wu_ref, wd_ref, *rest):
    out_gain_ref, o_ref = rest if len(rest) == 2 else (None, rest[0])
    x = x_ref[...]
    h = _rms(x, g_ref[...]).astype(BF16)
    gate = _dot(h, wg_ref[...])
    up = _dot(h, wu_ref[...])
    act = (gate * jax.nn.sigmoid(gate) * up).astype(BF16)
    y = x + 0.5 * _dot(act, wd_ref[...])
    o_ref[...] = y if out_gain_ref is None else _rms(y, out_gain_ref[...])


def _const_spec(shape):
    nd = len(shape)
    return pl.BlockSpec(shape, lambda *_: (0,) * nd, pipeline_mode=pl.Buffered(1))


def _ffn(x2, gain, wg, wu, wd, tm, out_gain=None):
    m, d = x2.shape
    consts = [gain, wg, wu, wd] + ([] if out_gain is None else [out_gain])
    return pl.pallas_call(
        _ffn_kernel,
        grid=(m // tm,),
        in_specs=[pl.BlockSpec((tm, d), lambda i: (i, 0))] + [_const_spec(c.shape) for c in consts],
        out_specs=pl.BlockSpec((tm, d), lambda i: (i, 0)),
        out_shape=jax.ShapeDtypeStruct((m, d), F32),
        compiler_params=pltpu.CompilerParams(dimension_semantics=("arbitrary",),
                                             vmem_limit_bytes=VMEM_LIMIT),
        name="ffn",
    )(x2, *consts)


def _rwkv_proj_kernel(x_ref, gain_ref, mu_ref, wr_ref, wk_ref, wv_ref, w0_ref, w1_ref, w2_ref,
                      a0_ref, a1_ref, a2_ref, g1_ref, g2_ref, kk_ref, ka_ref, ones_ref,
                      r_out, lw_out, k_out, v_out, a_out, b_out, g_out, carry_ref):
    t = pl.program_id(1)

    @pl.when(t == 0)
    def _():
        carry_ref[...] = jnp.zeros_like(carry_ref)

    h = _rms(x_ref[0], gain_ref[...])
    tm = h.shape[0]
    row = lax.broadcasted_iota(jnp.int32, h.shape, 0)
    h_prev = jnp.where(row == 0, carry_ref[0:1, :], pltpu.roll(h, 1, axis=0))
    carry_ref[0:1, :] = h[tm - 1:tm, :]
    xx = h_prev - h
    mix = lambda i: (h + xx * mu_ref[i:i + 1, :]).astype(BF16)

    r = _dot(mix(0), wr_ref[...])
    k = _dot(mix(2), wk_ref[...])
    v = _dot(mix(3), wv_ref[...])
    zw = w0_ref[...] + _dot(jnp.tanh(_dot(mix(1), w1_ref[...])).astype(BF16), w2_ref[...])
    lw = -DECAY_SCALE * jax.nn.sigmoid(zw)
    a = jax.nn.sigmoid(a0_ref[...] + _dot(_dot(mix(4), a1_ref[...]).astype(BF16), a2_ref[...]))
    g = _dot(jax.nn.sigmoid(_dot(mix(5), g1_ref[...])).astype(BF16), g2_ref[...])

    kk = k * kk_ref[...]
    norm = jnp.sqrt(_head_sum(kk * kk, ones_ref[...]))
    kk = kk / jnp.maximum(norm, L2_EPS)

    r_out[0] = r
    lw_out[0] = lw
    k_out[0] = k * (1.0 + (a - 1.0) * ka_ref[...])
    v_out[0] = v
    a_out[0] = -kk
    b_out[0] = kk * a
    g_out[0] = g


def _rwkv_proj(x, gain, mu, wr, wk, wv, w0, w1, w2, a0, a1, a2, g1, g2, k_k, k_a, ones_bd, tm):
    b, t, d = x.shape
    tok = pl.BlockSpec((1, tm, d), lambda i, j: (i, j, 0))
    consts = [gain, mu, wr, wk, wv, w0, w1, w2, a0, a1, a2, g1, g2, k_k, k_a, ones_bd]
    return pl.pallas_call(
        _rwkv_proj_kernel,
        grid=(b, t // tm),
        in_specs=[tok] + [_const_spec(c.shape) for c in consts],
        out_specs=[tok] * 7,
        out_shape=[jax.ShapeDtypeStruct((b, t, d), F32)] * 7,
        scratch_shapes=[pltpu.VMEM((SUBLANES, d), F32)],
        compiler_params=pltpu.CompilerParams(dimension_semantics=("arbitrary", "arbitrary"),
                                             vmem_limit_bytes=VMEM_LIMIT),
        name="rwkv_proj",
    )(x, *consts)


def _stack_heads(x, blk):
    return jnp.concatenate(
        [jnp.where(blk == h, x, 0.0).astype(BF16) for h in range(HEADS_PER_GROUP)], axis=0)


def _scan_chunk(r, lw, k, v, a, b, tri, states):
    c, d = r.shape
    assert HEADS_PER_GROUP * c == GROUP_LANES
    groups = range(d // GROUP_LANES)
    gl = GROUP_LANES
    per_group = lambda x: [x[:, g * gl:(g + 1) * gl] for g in groups]

    hi, mid, lo = _split3(lw)
    cs = _dot(tri, hi) + _dot(tri, mid) + _dot(tri, lo)
    cs_end = cs[c - 1:c, :]
    e_pos = jnp.exp(cs)
    e_neg = jnp.exp(-cs)
    e_end = jnp.exp(cs_end - cs)
    r_t = per_group(r * e_pos)
    a_t = per_group(a * jnp.exp(cs - lw))
    k_t = per_group(k * e_neg)
    b_t = per_group(b * e_neg)
    k_h = per_group(k * e_end)
    b_h = per_group(b * e_end)
    g_end = per_group(jnp.exp(cs_end))
    v = per_group(v)

    rowi = lax.broadcasted_iota(jnp.int32, (c, gl), 0)
    lane = lax.broadcasted_iota(jnp.int32, (c, gl), 1)
    blk = lane // HEAD
    col = lane - blk * HEAD
    strict = col < rowi
    incl = col <= rowi
    eye = jnp.where(col == rowi, 1.0, 0.0)
    stack = lambda x: _stack_heads(x, blk)

    aa = [_dot_nt(jnp.concatenate([a_t[g], r_t[g]], axis=0).astype(BF16),
                  jnp.concatenate([stack(b_t[g]), stack(k_t[g])], axis=0)) for g in groups]
    p = [jnp.where(strict, aa[g][:c, :gl], 0.0) for g in groups]
    a_xk = [jnp.concatenate([jnp.where(strict, aa[g][:c, gl:], 0.0),
                             jnp.where(incl, aa[g][c:, gl:], 0.0)], axis=0).astype(BF16) for g in groups]
    a_rb = [jnp.where(incl, aa[g][c:, :gl], 0.0).astype(BF16) for g in groups]

    xkv = [_dot(a_xk[g], stack(v[g])) for g in groups]

    tinv = [eye + p[g] for g in groups]
    n_sq = c.bit_length() - 1
    p = [_dot(p[g].astype(BF16), stack(p[g])) for g in groups]
    for i in range(1, n_sq):
        if i < n_sq - 1:
            both = [_dot(jnp.concatenate([p[g], tinv[g]], axis=0).astype(BF16), stack(p[g])) for g in groups]
            p = [both[g][:c] for g in groups]
            tinv = [tinv[g] + both[g][c:] for g in groups]
        else:
            tinv = [tinv[g] + _dot(tinv[g].astype(BF16), stack(p[g])) for g in groups]

    tx = [_dot(tinv[g].astype(BF16), jnp.concatenate([stack(a_t[g]), stack(xkv[g][:c])], axis=1))
          for g in groups]
    a_hat = [tx[g][:, :gl] for g in groups]
    u0 = [tx[g][:, gl:] for g in groups]
    ry = [_dot(a_rb[g], jnp.concatenate([stack(a_hat[g]), stack(u0[g])], axis=1)) for g in groups]

    s0b = [states[g].astype(BF16) for g in groups]
    y = [_dot_nt((r_t[g] + ry[g][:, :gl]).astype(BF16), s0b[g]) + ry[g][:, gl:] + xkv[g][c:] for g in groups]
    gmat = [_dot_tn(a_hat[g].astype(BF16), b_h[g].astype(BF16)) for g in groups]
    hmat = [_dot_tn(jnp.concatenate([u0[g], v[g]], axis=0).astype(BF16),
                    jnp.concatenate([b_h[g], k_h[g]], axis=0).astype(BF16)) for g in groups]
    srow = lax.broadcasted_iota(jnp.int32, (gl, gl), 0) // HEAD
    scol = lax.broadcasted_iota(jnp.int32, (gl, gl), 1) // HEAD
    on_diag = srow == scol
    new_states = [jnp.where(on_diag, states[g] * g_end[g] + _dot(s0b[g], gmat[g].astype(BF16)) + hmat[g], 0.0)
                  for g in groups]
    return jnp.concatenate(y, axis=-1), new_states


def _rwkv_scan_kernel(r_ref, lw_ref, k_ref, v_ref, a_ref, b_ref, tri_ref, y_ref, s_ref):
    @pl.when(pl.program_id(1) == 0)
    def _():
        s_ref[...] = jnp.zeros_like(s_ref)

    n_groups = s_ref.shape[0]
    y, new_states = _scan_chunk(r_ref[0], lw_ref[0], k_ref[0], v_ref[0], a_ref[0], b_ref[0], tri_ref[...],
                                [s_ref[g] for g in range(n_groups)])
    y_ref[0] = y
    for g in range(n_groups):
        s_ref[g] = new_states[g]


def _rwkv_scan(r, lw, k, v, a, b, tri):
    bsz, t, d = r.shape
    tok = pl.BlockSpec((1, CHUNK, d), lambda i, j: (i, j, 0))
    return pl.pallas_call(
        _rwkv_scan_kernel,
        grid=(bsz, t // CHUNK),
        in_specs=[tok] * 6 + [_const_spec(tri.shape)],
        out_specs=tok,
        out_shape=jax.ShapeDtypeStruct((bsz, t, d), F32),
        scratch_shapes=[pltpu.VMEM((d // GROUP_LANES, GROUP_LANES, GROUP_LANES), F32)],
        compiler_params=pltpu.CompilerParams(dimension_semantics=("arbitrary", "arbitrary"),
                                             vmem_limit_bytes=VMEM_LIMIT),
        name="rwkv_scan",
    )(r, lw, k, v, a, b, tri)


def _rwkv_out_kernel(x_ref, y_ref, r_ref, k_ref, v_ref, g_ref, rk_ref, lng_ref, lnb_ref, wo_ref, ones_ref,
                     o_ref):
    ones_bd = ones_ref[...]
    y = y_ref[...]
    mean = _head_sum(y, ones_bd) * (1.0 / HEAD)
    dlt = y - mean
    var = _head_sum(dlt * dlt, ones_bd) * (1.0 / HEAD)
    yn = dlt * lax.rsqrt(var + GN_EPS) * lng_ref[...] + lnb_ref[...]
    bonus = _head_sum(r_ref[...] * k_ref[...] * rk_ref[...], ones_bd) * v_ref[...]
    out = ((yn + bonus) * g_ref[...]).astype(BF16)
    o_ref[...] = x_ref[...] + _dot(out, wo_ref[...])


def _rwkv_out(x2, y2, r2, k2, v2, g2, r_k, ln_g, ln_b, wo, ones_bd, tm):
    m, d = x2.shape
    tok = pl.BlockSpec((tm, d), lambda i: (i, 0))
    consts = [r_k, ln_g, ln_b, wo, ones_bd]
    return pl.pallas_call(
        _rwkv_out_kernel,
        grid=(m // tm,),
        in_specs=[tok] * 6 + [_const_spec(c.shape) for c in consts],
        out_specs=tok,
        out_shape=jax.ShapeDtypeStruct((m, d), F32),
        compiler_params=pltpu.CompilerParams(dimension_semantics=("arbitrary",),
                                             vmem_limit_bytes=VMEM_LIMIT),
        name="rwkv_out",
    )(x2, y2, r2, k2, v2, g2, *consts)


CONV_ROWS = 32
CONV_LANES = 512


def _conv_kernel(x_ref, gain_ref, win_ref, bin_ref, dw_ref, dwb_ref, lng_ref, lnb_ref, wout_ref, bout_ref,
                 o_ref, ext_ref, sh_ref, cv_ref):
    tm, d = o_ref.shape[1], o_ref.shape[2]

    @pl.when(pl.program_id(1) == 0)
    def _():
        ext_ref[0:CONV_HALO, :] = jnp.zeros((CONV_HALO, d), F32)

    x = x_ref[0]
    h = _rms(x, gain_ref[...]).astype(BF16)
    z = _dot(h, win_ref[...]) + bin_ref[...]
    ext_ref[CONV_HALO:CONV_HALO + tm, :] = z[:, :d] * jax.nn.sigmoid(z[:, d:])

    off = CONV_HALO - (CONV_WIDTH - 1)
    sh_rows = sh_ref.shape[1]
    for s in range(1, SUBLANES):
        sh_ref[s] = ext_ref[s:s + sh_rows, :]

    def row_block(rb, carry):
        base = pl.multiple_of(rb * CONV_ROWS, CONV_ROWS)
        lanes = min(CONV_LANES, d)
        for lb in range(d // lanes):
            ls = slice(lb * lanes, (lb + 1) * lanes)
            acc = jnp.broadcast_to(dwb_ref[:, ls], (CONV_ROWS, lanes))
            for j in range(CONV_WIDTH):
                q, s = divmod(off + j, SUBLANES)
                rows = pl.ds(base + q * SUBLANES, CONV_ROWS)
                win = ext_ref[rows, ls] if s == 0 else sh_ref[s, rows, ls]
                acc = acc + dw_ref[j:j + 1, ls] * win
            cv_ref[pl.ds(base, CONV_ROWS), ls] = acc
        return carry

    lax.fori_loop(0, tm // CONV_ROWS, row_block, 0)
    ext_ref[0:CONV_HALO, :] = ext_ref[tm:tm + CONV_HALO, :]

    cv = cv_ref[...]
    mean = jnp.mean(cv, axis=-1, keepdims=True)
    dlt = cv - mean
    var = jnp.mean(dlt * dlt, axis=-1, keepdims=True)
    hn = dlt * lax.rsqrt(var + LN_EPS) * lng_ref[...] + lnb_ref[...]
    act = (hn * jax.nn.sigmoid(hn)).astype(BF16)
    o_ref[0] = x + _dot(act, wout_ref[...]) + bout_ref[...]


def _conv(x, gain, w_in, b_in, dw, dw_b, ln_g, ln_b, w_out, b_out, tm):
    b, t, d = x.shape
    tok = pl.BlockSpec((1, tm, d), lambda i, j: (i, j, 0))
    consts = [gain, w_in, b_in, dw, dw_b, ln_g, ln_b, w_out, b_out]
    sh_rows = tm + CONV_HALO - SUBLANES
    return pl.pallas_call(
        _conv_kernel,
        grid=(b, t // tm),
        in_specs=[tok] + [_const_spec(c.shape) for c in consts],
        out_specs=tok,
        out_shape=jax.ShapeDtypeStruct((b, t, d), F32),
        scratch_shapes=[pltpu.VMEM((tm + CONV_HALO, d), F32),
                        pltpu.VMEM((SUBLANES, sh_rows, d), F32),
                        pltpu.VMEM((tm, d), F32)],
        compiler_params=pltpu.CompilerParams(dimension_semantics=("arbitrary", "arbitrary"),
                                             vmem_limit_bytes=VMEM_LIMIT),
        name="conformer_conv",
    )(x, *consts)


def _row(v):
    return v.reshape(1, -1)


def kernel(x, norm_gains, ffn_w_gate, ffn_w_up, ffn_w_down, rwkv_mu, rwkv_w_rkv, rwkv_w0, rwkv_w1, rwkv_w2, rwkv_a0, rwkv_a1, rwkv_a2, rwkv_g1, rwkv_g2, rwkv_k_k, rwkv_k_a, rwkv_r_k, rwkv_ln_gain, rwkv_ln_bias, rwkv_w_out, conv_w_in, conv_b_in, conv_dw, conv_dw_b, conv_ln_gain, conv_ln_bias, conv_w_out, conv_b_out, final_norm):
    bsz, t, d = x.shape
    depth = norm_gains.shape[0]
    m = bsz * t
    tm_ffn = min(512, m)
    tm_tok = min(256, t)
    assert d % GROUP_LANES == 0 and t % CHUNK == 0 and t % tm_tok == 0 and m % tm_ffn == 0

    bf = lambda w: w.astype(BF16)
    blk = jnp.arange(GROUP_LANES, dtype=jnp.int32) // HEAD
    ones_bd = (blk[:, None] == blk[None, :]).astype(BF16)
    idx = jnp.arange(CHUNK, dtype=jnp.int32)
    tri = (idx[:, None] >= idx[None, :]).astype(BF16)

    def ffn(xc, i, s):
        last = i == depth - 1 and s == 1
        out = _ffn(xc.reshape(m, d), _row(norm_gains[i, 2 * s]), bf(ffn_w_gate[i, s]), bf(ffn_w_up[i, s]),
                   bf(ffn_w_down[i, s]), tm_ffn, out_gain=_row(final_norm) if last else None)
        return out.reshape(bsz, t, d)

    for i in range(depth):
        x = ffn(x, i, 0)
        j = i // 2
        gain = _row(norm_gains[i, 1])
        if i % 2 == 0:
            r, lw, k, v, a, b, g = _rwkv_proj(
                x, gain, rwkv_mu[j], bf(rwkv_w_rkv[j, 0]), bf(rwkv_w_rkv[j, 1]), bf(rwkv_w_rkv[j, 2]),
                _row(rwkv_w0[j]), bf(rwkv_w1[j]), bf(rwkv_w2[j]), _row(rwkv_a0[j]), bf(rwkv_a1[j]),
                bf(rwkv_a2[j]), bf(rwkv_g1[j]), bf(rwkv_g2[j]), _row(rwkv_k_k[j]), _row(rwkv_k_a[j]),
                ones_bd, tm_tok)
            y = _rwkv_scan(r, lw, k, v, a, b, tri)
            f2 = lambda z: z.reshape(m, d)
            x = _rwkv_out(f2(x), f2(y), f2(r), f2(k), f2(v), f2(g), _row(rwkv_r_k[j]), _row(rwkv_ln_gain[j]),
                          _row(rwkv_ln_bias[j]), bf(rwkv_w_out[j]), ones_bd, tm_ffn).reshape(bsz, t, d)
        else:
            x = _conv(x, gain, bf(conv_w_in[j]), _row(conv_b_in[j]), conv_dw[j], _row(conv_dw_b[j]),
                      _row(conv_ln_gain[j]), _row(conv_ln_bias[j]), bf(conv_w_out[j]), _row(conv_b_out[j]),
                      tm_tok)
        x = ffn(x, i, 1)
    return x
```

```python
import jax
import jax.numpy as jnp
from jax import lax
from jax.experimental import pallas as pl
from jax.experimental.pallas import tpu as pltpu

F32 = jnp.float32
BF16 = jnp.bfloat16

HEAD = 64
GROUP_LANES = 256
HEADS_PER_GROUP = GROUP_LANES // HEAD
CHUNK = 64
SCAN_ROWS = 2 * CHUNK
CONV_WIDTH = 31
CONV_HALO = 32
CONV_SUB = 128
CONV_ROWS = 32
CONV_LANES = 256
SUBLANES = 8
RMS_EPS = 1e-6
LN_EPS = 1e-5
GN_EPS = 64e-5
L2_EPS = 1e-12
DECAY_SCALE = 0.6065306597126334
VMEM_LIMIT = 56 * 1024 * 1024


def _rms(x, gain):
    ms = jnp.mean(x * x, axis=-1, keepdims=True)
    return x * lax.rsqrt(ms + RMS_EPS) * gain


def _dot(a, b):
    return jnp.dot(a, b, preferred_element_type=F32)


def _dot_nt(a, b):
    return lax.dot_general(a, b, (((1,), (1,)), ((), ())), preferred_element_type=F32)


def _dot_tn(a, b):
    return lax.dot_general(a, b, (((0,), (0,)), ((), ())), preferred_element_type=F32)


def _split2(x):
    hi = x.astype(BF16)
    lo = (x - hi.astype(F32)).astype(BF16)
    return hi, lo


def _split3(x):
    hi = x.astype(BF16)
    r1 = x - hi.astype(F32)
    mid = r1.astype(BF16)
    lo = (r1 - mid.astype(F32)).astype(BF16)
    return hi, mid, lo


def _head_sum(x, ones_bd):
    hi, lo = _split2(x)
    outs = []
    for s in range(x.shape[-1] // GROUP_LANES):
        sl = slice(s * GROUP_LANES, (s + 1) * GROUP_LANES)
        outs.append(_dot(hi[:, sl], ones_bd) + _dot(lo[:, sl], ones_bd))
    return jnp.concatenate(outs, axis=-1)


def _const_spec(shape):
    nd = len(shape)
    return pl.BlockSpec(shape, lambda *_: (0,) * nd, pipeline_mode=pl.Buffered(1))


def _ffn_kernel(x_ref, g_ref, wg_ref, wu_ref, wd_ref, *rest):
    out_gain_ref, o_ref = rest if len(rest) == 2 else (None, rest[0])
    x = x_ref[...]
    h = _rms(x, g_ref[...]).astype(BF16)
    gate = _dot(h, wg_ref[...])
    up = _dot(h, wu_ref[...])
    act = (gate * jax.nn.sigmoid(gate) * up).astype(BF16)
    y = x + 0.5 * _dot(act, wd_ref[...])
    o_ref[...] = y if out_gain_ref is None else _rms(y, out_gain_ref[...])


def _ffn(x2, gain, wg, wu, wd, tm, out_gain=None):
    m, d = x2.shape
    consts = [gain, wg, wu, wd] + ([] if out_gain is None else [out_gain])
    return pl.pallas_call(
        _ffn_kernel,
        grid=(m // tm,),
        in_specs=[pl.BlockSpec((tm, d), lambda i: (i, 0))] + [_const_spec(c.shape) for c in consts],
        out_specs=pl.BlockSpec((tm, d), lambda i: (i, 0)),
        out_shape=jax.ShapeDtypeStruct((m, d), F32),
        compiler_params=pltpu.CompilerParams(dimension_semantics=("arbitrary",),
                                             vmem_limit_bytes=VMEM_LIMIT),
        name="ffn",
    )(x2, *consts)


def _rwkv_proj_kernel(x_ref, gain_ref, mu_ref, wr_ref, wk_ref, wv_ref, w0_ref, w1_ref, w2_ref,
                      a0_ref, a1_ref, a2_ref, g1_ref, g2_ref, kk_ref, ka_ref, ones_ref,
                      r_out, lw_out, k_out, v_out, a_out, b_out, g_out, carry_ref):
    t = pl.program_id(1)

    @pl.when(t == 0)
    def _():
        carry_ref[...] = jnp.zeros_like(carry_ref)

    h = _rms(x_ref[0], gain_ref[...])
    tm = h.shape[0]
    row = lax.broadcasted_iota(jnp.int32, h.shape, 0)
    h_prev = jnp.where(row == 0, carry_ref[0:1, :], pltpu.roll(h, 1, axis=0))
    carry_ref[0:1, :] = h[tm - 1:tm, :]
    hb = h.astype(BF16)
    xxb = (h_prev - h).astype(BF16)
    mix = lambda i: hb + xxb * mu_ref[i:i + 1, :]

    r = _dot(mix(0), wr_ref[...])
    k = _dot(mix(2), wk_ref[...])
    v = _dot(mix(3), wv_ref[...])
    zw = w0_ref[...] + _dot(jnp.tanh(_dot(mix(1), w1_ref[...])).astype(BF16), w2_ref[...])
    lw = -DECAY_SCALE * jax.nn.sigmoid(zw)
    a = jax.nn.sigmoid(a0_ref[...] + _dot(_dot(mix(4), a1_ref[...]).astype(BF16), a2_ref[...]))
    g = _dot(jax.nn.sigmoid(_dot(mix(5), g1_ref[...])).astype(BF16), g2_ref[...])

    kk = k * kk_ref[...]
    norm = jnp.sqrt(_head_sum(kk * kk, ones_ref[...]))
    kk = kk / jnp.maximum(norm, L2_EPS)

    r_out[0] = r.astype(BF16)
    lw_out[0] = lw
    k_out[0] = (k * (1.0 + (a - 1.0) * ka_ref[...])).astype(BF16)
    v_out[0] = v.astype(BF16)
    a_out[0] = (-kk).astype(BF16)
    b_out[0] = (kk * a).astype(BF16)
    g_out[0] = g.astype(BF16)


def _rwkv_proj(x, gain, mu, wr, wk, wv, w0, w1, w2, a0, a1, a2, g1, g2, k_k, k_a, ones_bd, tm):
    b, t, d = x.shape
    tok = pl.BlockSpec((1, tm, d), lambda i, j: (i, j, 0))
    consts = [gain, mu, wr, wk, wv, w0, w1, w2, a0, a1, a2, g1, g2, k_k, k_a, ones_bd]
    out_dtypes = [BF16, F32, BF16, BF16, BF16, BF16, BF16]
    return pl.pallas_call(
        _rwkv_proj_kernel,
        grid=(b, t // tm),
        in_specs=[tok] + [_const_spec(c.shape) for c in consts],
        out_specs=[tok] * 7,
        out_shape=[jax.ShapeDtypeStruct((b, t, d), dt) for dt in out_dtypes],
        scratch_shapes=[pltpu.VMEM((SUBLANES, d), F32)],
        compiler_params=pltpu.CompilerParams(dimension_semantics=("arbitrary", "arbitrary"),
                                             vmem_limit_bytes=VMEM_LIMIT),
        name="rwkv_proj",
    )(x, *consts)


def _stack_heads(x, blk):
    return jnp.concatenate(
        [jnp.where(blk == h, x, 0.0).astype(BF16) for h in range(HEADS_PER_GROUP)], axis=0)


def _scan_block(r, lw, k, v, a, b, tri, states):
    rows, d = r.shape
    c, gl = CHUNK, GROUP_LANES
    assert HEADS_PER_GROUP * c == gl and rows % c == 0
    n_chunks, n_groups = rows // c, d // gl
    units = [(ci, g) for ci in range(n_chunks) for g in range(n_groups)]
    nu = range(len(units))

    hi, mid, lo = _split3(lw)
    cs = _dot(tri, hi) + _dot(tri, mid) + _dot(tri, lo)
    r_t, a_t, k_t, b_t, k_h, b_h, g_end, vv = ([] for _ in range(8))
    for ci in range(n_chunks):
        rs = slice(ci * c, (ci + 1) * c)
        cs_c = cs[rs]
        cs_end = cs_c[c - 1:c, :]
        e_neg = jnp.exp(-cs_c)
        e_end = jnp.exp(cs_end - cs_c)
        full = [(r_t, r[rs] * jnp.exp(cs_c)), (a_t, a[rs] * jnp.exp(cs_c - lw[rs])), (k_t, k[rs] * e_neg),
                (b_t, b[rs] * e_neg), (k_h, k[rs] * e_end), (b_h, b[rs] * e_end),
                (g_end, jnp.exp(cs_end)), (vv, v[rs])]
        for g in range(n_groups):
            for dst, val in full:
                dst.append(val[:, g * gl:(g + 1) * gl])

    rowi = lax.broadcasted_iota(jnp.int32, (c, gl), 0)
    lane = lax.broadcasted_iota(jnp.int32, (c, gl), 1)
    blk = lane // HEAD
    col = lane - blk * HEAD
    strict = col < rowi
    incl = col <= rowi
    eye = jnp.where(col == rowi, 1.0, 0.0)
    stack = lambda x: _stack_heads(x, blk)

    aa = [_dot_nt(jnp.concatenate([a_t[u], r_t[u]], axis=0).astype(BF16),
                  jnp.concatenate([stack(b_t[u]), stack(k_t[u])], axis=0)) for u in nu]
    p = [jnp.where(strict, aa[u][:c, :gl], 0.0) for u in nu]
    a_xk = [jnp.concatenate([jnp.where(strict, aa[u][:c, gl:], 0.0),
                             jnp.where(incl, aa[u][c:, gl:], 0.0)], axis=0).astype(BF16) for u in nu]
    a_rb = [jnp.where(incl, aa[u][c:, :gl], 0.0).astype(BF16) for u in nu]

    xkv = [_dot(a_xk[u], stack(vv[u])) for u in nu]

    tinv = [eye + p[u] for u in nu]
    n_sq = c.bit_length() - 1
    p = [_dot(p[u].astype(BF16), stack(p[u])) for u in nu]
    for i in range(1, n_sq):
        if i < n_sq - 1:
            both = [_dot(jnp.concatenate([p[u], tinv[u]], axis=0).astype(BF16), stack(p[u])) for u in nu]
            p = [both[u][:c] for u in nu]
            tinv = [tinv[u] + both[u][c:] for u in nu]
        else:
            tinv = [tinv[u] + _dot(tinv[u].astype(BF16), stack(p[u])) for u in nu]

    tx = [_dot(tinv[u].astype(BF16), jnp.concatenate([stack(a_t[u]), stack(xkv[u][:c])], axis=1)) for u in nu]
    ry = [_dot(a_rb[u], jnp.concatenate([stack(tx[u][:, :gl]), stack(tx[u][:, gl:])], axis=1)) for u in nu]
    ra_hat = [jnp.concatenate([r_t[u] + ry[u][:, :gl], tx[u][:, :gl]], axis=0).astype(BF16) for u in nu]
    y0 = [ry[u][:, gl:] + xkv[u][c:] for u in nu]
    bk_h = [jnp.concatenate([b_h[u], k_h[u]], axis=0).astype(BF16) for u in nu]

    srow = lax.broadcasted_iota(jnp.int32, (gl, gl), 0) // HEAD
    scol = lax.broadcasted_iota(jnp.int32, (gl, gl), 1) // HEAD
    on_diag = srow == scol
    states = list(states)
    ys = []
    for ci in range(n_chunks):
        row_y = []
        for g in range(n_groups):
            u = ci * n_groups + g
            s0 = states[g]
            nt = _dot_nt(ra_hat[u], s0.astype(BF16))
            row_y.append(nt[:c] + y0[u])
            uv = jnp.concatenate([nt[c:] + tx[u][:, gl:], vv[u]], axis=0).astype(BF16)
            states[g] = jnp.where(on_diag, s0 * g_end[u] + _dot_tn(uv, bk_h[u]), 0.0)
        ys.append(jnp.concatenate(row_y, axis=-1))
    return jnp.concatenate(ys, axis=0), states


def _rwkv_mix_kernel(x_ref, r_ref, lw_ref, k_ref, v_ref, a_ref, b_ref, g_ref, tri_ref, rk_ref, lng_ref,
                     lnb_ref, wo_ref, ones_ref, o_ref, s_ref):
    @pl.when(pl.program_id(1) == 0)
    def _():
        s_ref[...] = jnp.zeros_like(s_ref)

    n_groups = s_ref.shape[0]
    r, k, v = r_ref[0].astype(F32), k_ref[0].astype(F32), v_ref[0].astype(F32)
    y, new_states = _scan_block(r, lw_ref[0], k, v, a_ref[0].astype(F32), b_ref[0].astype(F32), tri_ref[...],
                                [s_ref[g] for g in range(n_groups)])
    for g in range(n_groups):
        s_ref[g] = new_states[g]

    ones_bd = ones_ref[...]
    mean = _head_sum(y, ones_bd) * (1.0 / HEAD)
    dlt = y - mean
    var = _head_sum(dlt * dlt, ones_bd) * (1.0 / HEAD)
    yn = dlt * lax.rsqrt(var + GN_EPS) * lng_ref[...] + lnb_ref[...]
    bonus = _head_sum(r * k * rk_ref[...], ones_bd) * v
    out = ((yn + bonus) * g_ref[0].astype(F32)).astype(BF16)
    o_ref[0] = x_ref[0] + _dot(out, wo_ref[...])


def _rwkv_mix(x, r, lw, k, v, a, b, g, tri, r_k, ln_g, ln_b, wo, ones_bd):
    bsz, t, d = x.shape
    tok = pl.BlockSpec((1, SCAN_ROWS, d), lambda i, j: (i, j, 0))
    consts = [tri, r_k, ln_g, ln_b, wo, ones_bd]
    return pl.pallas_call(
        _rwkv_mix_kernel,
        grid=(bsz, t // SCAN_ROWS),
        in_specs=[tok] * 8 + [_const_spec(c.shape) for c in consts],
        out_specs=tok,
        out_shape=jax.ShapeDtypeStruct((bsz, t, d), F32),
        scratch_shapes=[pltpu.VMEM((d // GROUP_LANES, GROUP_LANES, GROUP_LANES), F32)],
        compiler_params=pltpu.CompilerParams(dimension_semantics=("arbitrary", "arbitrary"),
                                             vmem_limit_bytes=VMEM_LIMIT),
        name="rwkv_mix",
    )(x, r, lw, k, v, a, b, g, *consts)


def _conv_kernel(x_ref, gain_ref, win_ref, bin_ref, dw_ref, dwb_ref, lng_ref, lnb_ref, wout_ref, bout_ref,
                 o_ref, ext_ref, sh_ref):
    tm, d = o_ref.shape[1], o_ref.shape[2]
    sub = min(CONV_SUB, tm)
    lanes = min(CONV_LANES, d)
    off = CONV_HALO - (CONV_WIDTH - 1)
    span = sub + CONV_HALO
    nrb = CONV_ROWS // SUBLANES

    @pl.when(pl.program_id(1) == 0)
    def _():
        ext_ref[0:CONV_HALO, :] = jnp.zeros((CONV_HALO, d), F32)

    for st in range(tm // sub):
        r0 = st * sub
        x = x_ref[0, r0:r0 + sub, :]
        h = _rms(x, gain_ref[...]).astype(BF16)
        z = _dot(h, win_ref[...]) + bin_ref[...]
        ext_ref[CONV_HALO + r0:CONV_HALO + r0 + sub, :] = z[:, :d] * jax.nn.sigmoid(z[:, d:])
        window = ext_ref[r0:r0 + span, :]
        for s in range(1, SUBLANES):
            sh_ref[s, r0:r0 + span - SUBLANES, :] = pltpu.roll(window, span - s, axis=0)[:span - SUBLANES]

        blocks = []
        for rb in range(sub // CONV_ROWS):
            lane_blocks = []
            for lb in range(d // lanes):
                ls = slice(lb * lanes, (lb + 1) * lanes)
                acc = jnp.broadcast_to(dwb_ref[:, ls], (CONV_ROWS, lanes)).reshape(nrb, SUBLANES, lanes)
                for j in range(CONV_WIDTH):
                    q, s = divmod(off + j, SUBLANES)
                    lo = r0 + rb * CONV_ROWS + q * SUBLANES
                    win = ext_ref[lo:lo + CONV_ROWS, ls] if s == 0 else sh_ref[s, lo:lo + CONV_ROWS, ls]
                    tap = dw_ref[j * SUBLANES:(j + 1) * SUBLANES, ls]
                    acc = acc + tap[None] * win.reshape(nrb, SUBLANES, lanes)
                lane_blocks.append(acc.reshape(CONV_ROWS, lanes))
            blocks.append(jnp.concatenate(lane_blocks, axis=-1))
        cv = jnp.concatenate(blocks, axis=0)

        mean = jnp.mean(cv, axis=-1, keepdims=True)
        dlt = cv - mean
        var = jnp.mean(dlt * dlt, axis=-1, keepdims=True)
        hn = dlt * lax.rsqrt(var + LN_EPS) * lng_ref[...] + lnb_ref[...]
        act = (hn * jax.nn.sigmoid(hn)).astype(BF16)
        o_ref[0, r0:r0 + sub, :] = x + _dot(act, wout_ref[...]) + bout_ref[...]

    ext_ref[0:CONV_HALO, :] = ext_ref[tm:tm + CONV_HALO, :]


def _conv(x, gain, w_in, b_in, dw8, dw_b, ln_g, ln_b, w_out, b_out, tm):
    b, t, d = x.shape
    tok = pl.BlockSpec((1, tm, d), lambda i, j: (i, j, 0))
    consts = [gain, w_in, b_in, dw8, dw_b, ln_g, ln_b, w_out, b_out]
    return pl.pallas_call(
        _conv_kernel,
        grid=(b, t // tm),
        in_specs=[tok] + [_const_spec(c.shape) for c in consts],
        out_specs=tok,
        out_shape=jax.ShapeDtypeStruct((b, t, d), F32),
        scratch_shapes=[pltpu.VMEM((tm + CONV_HALO, d), F32),
                        pltpu.VMEM((SUBLANES, tm + CONV_HALO - SUBLANES, d), F32)],
        compiler_params=pltpu.CompilerParams(dimension_semantics=("arbitrary", "arbitrary"),
                                             vmem_limit_bytes=VMEM_LIMIT),
        name="conformer_conv",
    )(x, *consts)


def _row(v):
    return v.reshape(1, -1)


def kernel(x, norm_gains, ffn_w_gate, ffn_w_up, ffn_w_down, rwkv_mu, rwkv_w_rkv, rwkv_w0, rwkv_w1, rwkv_w2, rwkv_a0, rwkv_a1, rwkv_a2, rwkv_g1, rwkv_g2, rwkv_k_k, rwkv_k_a, rwkv_r_k, rwkv_ln_gain, rwkv_ln_bias, rwkv_w_out, conv_w_in, conv_b_in, conv_dw, conv_dw_b, conv_ln_gain, conv_ln_bias, conv_w_out, conv_b_out, final_norm):
    bsz, t, d = x.shape
    depth = norm_gains.shape[0]
    m = bsz * t
    tm_ffn = min(512, m)
    tm_tok = min(256, t)
    assert depth >= 1 and d % GROUP_LANES == 0 and t % SCAN_ROWS == 0 and t % tm_tok == 0 and m % tm_ffn == 0

    bf = lambda w: w.astype(BF16)
    blk = jnp.arange(GROUP_LANES, dtype=jnp.int32) // HEAD
    ones_bd = (blk[:, None] == blk[None, :]).astype(BF16)
    idx = jnp.arange(SCAN_ROWS, dtype=jnp.int32)
    tri = ((idx[:, None] >= idx[None, :]) & (idx[:, None] // CHUNK == idx[None, :] // CHUNK)).astype(BF16)

    def ffn(xc, i, s):
        last = i == depth - 1 and s == 1
        out = _ffn(xc.reshape(m, d), _row(norm_gains[i, 2 * s]), bf(ffn_w_gate[i, s]), bf(ffn_w_up[i, s]),
                   bf(ffn_w_down[i, s]), tm_ffn, out_gain=_row(final_norm) if last else None)
        return out.reshape(bsz, t, d)

    for i in range(depth):
        x = ffn(x, i, 0)
        j = i // 2
        gain = _row(norm_gains[i, 1])
        if i % 2 == 0:
            r, lw, k, v, a, b, g = _rwkv_proj(
                x, gain, bf(rwkv_mu[j]), bf(rwkv_w_rkv[j, 0]), bf(rwkv_w_rkv[j, 1]), bf(rwkv_w_rkv[j, 2]),
                _row(rwkv_w0[j]), bf(rwkv_w1[j]), bf(rwkv_w2[j]), _row(rwkv_a0[j]), bf(rwkv_a1[j]),
                bf(rwkv_a2[j]), bf(rwkv_g1[j]), bf(rwkv_g2[j]), _row(rwkv_k_k[j]), _row(rwkv_k_a[j]),
                ones_bd, tm_tok)
            x = _rwkv_mix(x, r, lw, k, v, a, b, g, tri, _row(rwkv_r_k[j]), _row(rwkv_ln_gain[j]),
                          _row(rwkv_ln_bias[j]), bf(rwkv_w_out[j]), ones_bd)
        else:
            dw8 = jnp.repeat(conv_dw[j], SUBLANES, axis=0)
            x = _conv(x, gain, bf(conv_w_in[j]), _row(conv_b_in[j]), dw8, _row(conv_dw_b[j]),
                      _row(conv_ln_gain[j]), _row(conv_ln_bias[j]), bf(conv_w_out[j]), _row(conv_b_out[j]),
                      tm_tok)
        x = ffn(x, i, 1)
    return x
```

```python
import jax
import jax.numpy as jnp
from jax import lax
from jax.experimental import pallas as pl
from jax.experimental.pallas import tpu as pltpu

F32 = jnp.float32
BF16 = jnp.bfloat16

HEAD = 64
GROUP_LANES = 256
HEADS_PER_GROUP = GROUP_LANES // HEAD
CHUNK = 64
SCAN_BLOCK = 2 * CHUNK
SCAN_ROWS = 2 * SCAN_BLOCK
CONV_WIDTH = 31
CONV_HALO = 32
CONV_SUB = 128
CONV_ROWS = 32
CONV_LANES = 256
SUBLANES = 8
RMS_EPS = 1e-6
LN_EPS = 1e-5
GN_EPS = 64e-5
L2_EPS = 1e-12
DECAY_SCALE = 0.6065306597126334
VMEM_LIMIT = 56 * 1024 * 1024


def _rms(x, gain):
    ms = jnp.mean(x * x, axis=-1, keepdims=True)
    return x * lax.rsqrt(ms + RMS_EPS) * gain


def _dot(a, b):
    return jnp.dot(a, b, preferred_element_type=F32)


def _dot_nt(a, b):
    return lax.dot_general(a, b, (((1,), (1,)), ((), ())), preferred_element_type=F32)


def _dot_tn(a, b):
    return lax.dot_general(a, b, (((0,), (0,)), ((), ())), preferred_element_type=F32)


def _split2(x):
    hi = x.astype(BF16)
    lo = (x - hi.astype(F32)).astype(BF16)
    return hi, lo


def _split3(x):
    hi = x.astype(BF16)
    r1 = x - hi.astype(F32)
    mid = r1.astype(BF16)
    lo = (r1 - mid.astype(F32)).astype(BF16)
    return hi, mid, lo


def _head_sums(xs, ones_bd):
    outs = []
    for x in xs:
        hi, lo = _split2(x)
        slabs = [slice(s * GROUP_LANES, (s + 1) * GROUP_LANES) for s in range(x.shape[-1] // GROUP_LANES)]
        outs.append(jnp.concatenate([_dot(hi[:, sl], ones_bd) + _dot(lo[:, sl], ones_bd) for sl in slabs],
                                    axis=-1))
    return outs


def _const_spec(shape):
    nd = len(shape)
    return pl.BlockSpec(shape, lambda *_: (0,) * nd, pipeline_mode=pl.Buffered(1))


def _ffn_kernel(x_ref, g_ref, wg_ref, wu_ref, wd_ref, *rest):
    out_gain_ref, o_ref = rest if len(rest) == 2 else (None, rest[0])
    x = x_ref[...]
    h = _rms(x, g_ref[...]).astype(BF16)
    gate = _dot(h, wg_ref[...])
    up = _dot(h, wu_ref[...])
    act = (gate * jax.nn.sigmoid(gate) * up).astype(BF16)
    y = x + 0.5 * _dot(act, wd_ref[...])
    o_ref[...] = y if out_gain_ref is None else _rms(y, out_gain_ref[...])


def _ffn(x2, gain, wg, wu, wd, tm, out_gain=None):
    m, d = x2.shape
    consts = [gain, wg, wu, wd] + ([] if out_gain is None else [out_gain])
    return pl.pallas_call(
        _ffn_kernel,
        grid=(m // tm,),
        in_specs=[pl.BlockSpec((tm, d), lambda i: (i, 0))] + [_const_spec(c.shape) for c in consts],
        out_specs=pl.BlockSpec((tm, d), lambda i: (i, 0)),
        out_shape=jax.ShapeDtypeStruct((m, d), F32),
        compiler_params=pltpu.CompilerParams(dimension_semantics=("arbitrary",),
                                             vmem_limit_bytes=VMEM_LIMIT),
        name="ffn",
    )(x2, *consts)


def _rwkv_proj_kernel(x_ref, gain_ref, mu_ref, wr_ref, wk_ref, wv_ref, w0_ref, w1_ref, w2_ref,
                      a0_ref, a1_ref, a2_ref, g1_ref, g2_ref, kk_ref, ka_ref, ones_ref,
                      r_out, lw_out, k_out, v_out, a_out, b_out, g_out, carry_ref):
    t = pl.program_id(1)

    @pl.when(t == 0)
    def _():
        carry_ref[...] = jnp.zeros_like(carry_ref)

    h = _rms(x_ref[0], gain_ref[...])
    tm = h.shape[0]
    row = lax.broadcasted_iota(jnp.int32, h.shape, 0)
    h_prev = jnp.where(row == 0, carry_ref[0:1, :], pltpu.roll(h, 1, axis=0))
    carry_ref[0:1, :] = h[tm - 1:tm, :]
    hb = h.astype(BF16)
    xxb = (h_prev - h).astype(BF16)
    mix = lambda i: hb + xxb * mu_ref[i:i + 1, :]

    r = _dot(mix(0), wr_ref[...])
    k = _dot(mix(2), wk_ref[...])
    v = _dot(mix(3), wv_ref[...])
    zw = w0_ref[...] + _dot(jnp.tanh(_dot(mix(1), w1_ref[...])).astype(BF16), w2_ref[...])
    lw = -DECAY_SCALE * jax.nn.sigmoid(zw)
    a = jax.nn.sigmoid(a0_ref[...] + _dot(_dot(mix(4), a1_ref[...]).astype(BF16), a2_ref[...]))
    g = _dot(jax.nn.sigmoid(_dot(mix(5), g1_ref[...])).astype(BF16), g2_ref[...])

    kk = k * kk_ref[...]
    norm = jnp.sqrt(_head_sums([kk * kk], ones_ref[...])[0])
    kk = kk / jnp.maximum(norm, L2_EPS)

    r_out[0] = r.astype(BF16)
    lw_out[0] = lw
    k_out[0] = (k * (1.0 + (a - 1.0) * ka_ref[...])).astype(BF16)
    v_out[0] = v.astype(BF16)
    a_out[0] = (-kk).astype(BF16)
    b_out[0] = (kk * a).astype(BF16)
    g_out[0] = g.astype(BF16)


def _rwkv_proj(x, gain, mu, wr, wk, wv, w0, w1, w2, a0, a1, a2, g1, g2, k_k, k_a, ones_bd, tm):
    b, t, d = x.shape
    tok = pl.BlockSpec((1, tm, d), lambda i, j: (i, j, 0))
    consts = [gain, mu, wr, wk, wv, w0, w1, w2, a0, a1, a2, g1, g2, k_k, k_a, ones_bd]
    out_dtypes = [BF16, F32, BF16, BF16, BF16, BF16, BF16]
    return pl.pallas_call(
        _rwkv_proj_kernel,
        grid=(b, t // tm),
        in_specs=[tok] + [_const_spec(c.shape) for c in consts],
        out_specs=[tok] * 7,
        out_shape=[jax.ShapeDtypeStruct((b, t, d), dt) for dt in out_dtypes],
        scratch_shapes=[pltpu.VMEM((SUBLANES, d), F32)],
        compiler_params=pltpu.CompilerParams(dimension_semantics=("arbitrary", "arbitrary"),
                                             vmem_limit_bytes=VMEM_LIMIT),
        name="rwkv_proj",
    )(x, *consts)


def _stack_heads(x, blk):
    return jnp.concatenate(
        [jnp.where(blk == h, x, 0.0).astype(BF16) for h in range(HEADS_PER_GROUP)], axis=0)


_SCALED_NAMES = ("r_t", "a_t", "k_t", "b_t", "k_h", "b_h", "g_end", "vv")


def _scan_scale_tasks(load, tri, scaled):
    c, gl = CHUNK, GROUP_LANES
    rows = tri.shape[0]
    for name in _SCALED_NAMES:
        scaled[name] = []
    cs_box = []

    def cumulative_decay():
        hi, mid, lo = _split3(load("lw", slice(0, rows)))
        cs_box.append(_dot(tri, hi) + _dot(tri, mid) + _dot(tri, lo))

    def scale_chunk(ci):
        rs = slice(ci * c, (ci + 1) * c)
        cs_c = cs_box[0][rs]
        cs_end = cs_c[c - 1:c, :]
        e_neg = jnp.exp(-cs_c)
        e_end = jnp.exp(cs_end - cs_c)
        k, b = load("k", rs), load("b", rs)
        full = dict(r_t=load("r", rs) * jnp.exp(cs_c), a_t=load("a", rs) * jnp.exp(cs_c - load("lw", rs)),
                    k_t=k * e_neg, b_t=b * e_neg, k_h=k * e_end, b_h=b * e_end,
                    g_end=jnp.exp(cs_end), vv=load("v", rs))
        for g in range(cs_c.shape[1] // gl):
            for name in _SCALED_NAMES:
                scaled[name].append(full[name][:, g * gl:(g + 1) * gl])

    return [cumulative_decay] + [lambda ci=ci: scale_chunk(ci) for ci in range(rows // c)]


def _scan_local(scaled, hook):
    r_t, a_t, k_t, b_t, k_h, b_h, g_end, vv = (scaled[name] for name in _SCALED_NAMES)
    c, gl = CHUNK, GROUP_LANES
    assert HEADS_PER_GROUP * c == gl
    nu = range(len(r_t))

    rowi = lax.broadcasted_iota(jnp.int32, (c, gl), 0)
    lane = lax.broadcasted_iota(jnp.int32, (c, gl), 1)
    blk = lane // HEAD
    col = lane - blk * HEAD
    strict = col < rowi
    incl = col <= rowi
    eye = jnp.where(col == rowi, 1.0, 0.0)
    stack = lambda x: _stack_heads(x, blk)

    aa = [_dot_nt(jnp.concatenate([a_t[u], r_t[u]], axis=0).astype(BF16),
                  jnp.concatenate([stack(b_t[u]), stack(k_t[u])], axis=0)) for u in nu]
    p = [jnp.where(strict, aa[u][:c, :gl], 0.0) for u in nu]
    a_xk = [jnp.concatenate([jnp.where(strict, aa[u][:c, gl:], 0.0),
                             jnp.where(incl, aa[u][c:, gl:], 0.0)], axis=0).astype(BF16) for u in nu]
    a_rb = [jnp.where(incl, aa[u][c:, :gl], 0.0) for u in nu]
    hook()

    xkv = [_dot(a_xk[u], stack(vv[u])) for u in nu]
    hook()

    n_sq = c.bit_length() - 1
    assert n_sq >= 2
    both = [_dot(jnp.concatenate([p[u], a_rb[u]], axis=0).astype(BF16), stack(p[u])) for u in nu]
    tq = [jnp.concatenate([eye + p[u], a_rb[u] + both[u][c:]], axis=0) for u in nu]
    p = [both[u][:c] for u in nu]
    hook()
    for i in range(1, n_sq):
        if i < n_sq - 1:
            both = [_dot(jnp.concatenate([p[u], tq[u]], axis=0).astype(BF16), stack(p[u])) for u in nu]
            p = [both[u][:c] for u in nu]
            tq = [tq[u] + both[u][c:] for u in nu]
        else:
            tq = [tq[u] + _dot(tq[u].astype(BF16), stack(p[u])) for u in nu]
        hook()

    tx = [_dot(tq[u].astype(BF16), jnp.concatenate([stack(a_t[u]), stack(xkv[u][:c])], axis=1)) for u in nu]
    hook()
    return dict(
        ra_hat=[jnp.concatenate([r_t[u] + tx[u][c:, :gl], tx[u][:c, :gl]], axis=0).astype(BF16) for u in nu],
        y0=[tx[u][c:, gl:] + xkv[u][c:] for u in nu],
        u0=[tx[u][:c, gl:] for u in nu],
        bk_h=[jnp.concatenate([b_h[u], k_h[u]], axis=0).astype(BF16) for u in nu],
        vv=vv, g_end=g_end)


def _scan_state_tasks(loc, states, ys):
    c, gl = CHUNK, GROUP_LANES
    n_groups = len(states)
    srow = lax.broadcasted_iota(jnp.int32, (gl, gl), 0) // HEAD
    scol = lax.broadcasted_iota(jnp.int32, (gl, gl), 1) // HEAD
    on_diag = srow == scol
    tasks = []
    for ci in range(len(loc["y0"]) // n_groups):
        us = [ci * n_groups + g for g in range(n_groups)]
        nt = []

        def read_state(us=us, nt=nt):
            nt[:] = [_dot_nt(loc["ra_hat"][u], states[g].astype(BF16)) for g, u in enumerate(us)]
            ys.append(jnp.concatenate([nt[g][:c] + loc["y0"][u] for g, u in enumerate(us)], axis=-1))

        def advance_state(us=us, nt=nt):
            uv = [jnp.concatenate([nt[g][c:] + loc["u0"][u], loc["vv"][u]], axis=0).astype(BF16)
                  for g, u in enumerate(us)]
            states[:] = [jnp.where(on_diag, states[g] * loc["g_end"][u] + _dot_tn(uv[g], loc["bk_h"][u]), 0.0)
                         for g, u in enumerate(us)]

        tasks += [read_state, advance_state]
    return tasks


def _rwkv_mix_kernel(x_ref, r_ref, lw_ref, k_ref, v_ref, a_ref, b_ref, g_ref, tri_ref, rk_ref, lng_ref,
                     lnb_ref, wo_ref, ones_ref, o_ref, s_ref):
    @pl.when(pl.program_id(1) == 0)
    def _():
        s_ref[...] = jnp.zeros_like(s_ref)

    n_groups = s_ref.shape[0]
    n_blocks = o_ref.shape[1] // SCAN_BLOCK
    f32_rows = lambda ref, rs: ref[0, rs, :].astype(F32)
    token_refs = dict(r=r_ref, lw=lw_ref, k=k_ref, v=v_ref, a=a_ref, b=b_ref)
    states = [s_ref[g] for g in range(n_groups)]
    pending = []

    def hook():
        if pending:
            pending.pop(0)()

    def block_loader(blk_i):
        base = blk_i * SCAN_BLOCK
        return lambda name, rs: f32_rows(token_refs[name], slice(base + rs.start, base + rs.stop))

    scaled = [dict() for _ in range(n_blocks)]
    scale_tasks = [_scan_scale_tasks(block_loader(i), tri_ref[...], scaled[i]) for i in range(n_blocks)]
    ys = [[] for _ in range(n_blocks)]
    for task in scale_tasks[0]:
        task()
    for blk_i in range(n_blocks):
        next_scale = scale_tasks[blk_i + 1] if blk_i + 1 < n_blocks else []
        pending += next_scale
        loc = _scan_local(scaled[blk_i], hook)
        while any(task in pending for task in next_scale):
            pending.pop(0)()
        pending += _scan_state_tasks(loc, states, ys[blk_i])

    ones_bd = ones_ref[...]
    for blk_i in range(n_blocks):
        rs = slice(blk_i * SCAN_BLOCK, (blk_i + 1) * SCAN_BLOCK)
        while len(ys[blk_i]) < SCAN_BLOCK // CHUNK:
            pending.pop(0)()
        y = jnp.concatenate(ys[blk_i], axis=0)
        r, k, v = f32_rows(r_ref, rs), f32_rows(k_ref, rs), f32_rows(v_ref, rs)
        y_sum, rk_sum = _head_sums([y, r * k * rk_ref[...]], ones_bd)
        hook()
        dlt = y - y_sum * (1.0 / HEAD)
        var = _head_sums([dlt * dlt], ones_bd)[0] * (1.0 / HEAD)
        hook()
        yn = dlt * lax.rsqrt(var + GN_EPS) * lng_ref[...] + lnb_ref[...]
        out = ((yn + rk_sum * v) * f32_rows(g_ref, rs)).astype(BF16)
        hook()
        o_ref[0, rs, :] = x_ref[0, rs, :] + _dot(out, wo_ref[...])
        hook()
    while pending:
        pending.pop(0)()
    for g in range(n_groups):
        s_ref[g] = states[g]


def _rwkv_mix(x, r, lw, k, v, a, b, g, tri, r_k, ln_g, ln_b, wo, ones_bd):
    bsz, t, d = x.shape
    tok = pl.BlockSpec((1, SCAN_ROWS, d), lambda i, j: (i, j, 0))
    consts = [tri, r_k, ln_g, ln_b, wo, ones_bd]
    return pl.pallas_call(
        _rwkv_mix_kernel,
        grid=(bsz, t // SCAN_ROWS),
        in_specs=[tok] * 8 + [_const_spec(c.shape) for c in consts],
        out_specs=tok,
        out_shape=jax.ShapeDtypeStruct((bsz, t, d), F32),
        scratch_shapes=[pltpu.VMEM((d // GROUP_LANES, GROUP_LANES, GROUP_LANES), F32)],
        compiler_params=pltpu.CompilerParams(dimension_semantics=("arbitrary", "arbitrary"),
                                             vmem_limit_bytes=VMEM_LIMIT),
        name="rwkv_mix",
    )(x, r, lw, k, v, a, b, g, *consts)


def _conv_kernel(x_ref, gain_ref, win_ref, bin_ref, dw_ref, dwb_ref, lng_ref, lnb_ref, wout_ref, bout_ref,
                 o_ref, ext_ref, sh_ref):
    tm, d = o_ref.shape[1], o_ref.shape[2]
    sub = min(CONV_SUB, tm)
    lanes = min(CONV_LANES, d)
    off = CONV_HALO - (CONV_WIDTH - 1)
    span = sub + CONV_HALO
    nrb = CONV_ROWS // SUBLANES

    @pl.when(pl.program_id(1) == 0)
    def _():
        ext_ref[0:CONV_HALO, :] = jnp.zeros((CONV_HALO, d), F32)

    for st in range(tm // sub):
        r0 = st * sub
        x = x_ref[0, r0:r0 + sub, :]
        h = _rms(x, gain_ref[...]).astype(BF16)
        z = _dot(h, win_ref[...]) + bin_ref[...]
        ext_ref[CONV_HALO + r0:CONV_HALO + r0 + sub, :] = z[:, :d] * jax.nn.sigmoid(z[:, d:])
        window = ext_ref[r0:r0 + span, :]
        for s in range(1, SUBLANES):
            sh_ref[s, r0:r0 + span - SUBLANES, :] = pltpu.roll(window, span - s, axis=0)[:span - SUBLANES]

        blocks = []
        for rb in range(sub // CONV_ROWS):
            lane_blocks = []
            for lb in range(d // lanes):
                ls = slice(lb * lanes, (lb + 1) * lanes)
                acc = jnp.broadcast_to(dwb_ref[:, ls], (CONV_ROWS, lanes)).reshape(nrb, SUBLANES, lanes)
                for j in range(CONV_WIDTH):
                    q, s = divmod(off + j, SUBLANES)
                    lo = r0 + rb * CONV_ROWS + q * SUBLANES
                    win = ext_ref[lo:lo + CONV_ROWS, ls] if s == 0 else sh_ref[s, lo:lo + CONV_ROWS, ls]
                    tap = dw_ref[j * SUBLANES:(j + 1) * SUBLANES, ls]
                    acc = acc + tap[None] * win.reshape(nrb, SUBLANES, lanes)
                lane_blocks.append(acc.reshape(CONV_ROWS, lanes))
            blocks.append(jnp.concatenate(lane_blocks, axis=-1))
        cv = jnp.concatenate(blocks, axis=0)

        mean = jnp.mean(cv, axis=-1, keepdims=True)
        dlt = cv - mean
        var = jnp.mean(dlt * dlt, axis=-1, keepdims=True)
        hn = dlt * lax.rsqrt(var + LN_EPS) * lng_ref[...] + lnb_ref[...]
        act = (hn * jax.nn.sigmoid(hn)).astype(BF16)
        o_ref[0, r0:r0 + sub, :] = x + _dot(act, wout_ref[...]) + bout_ref[...]

    ext_ref[0:CONV_HALO, :] = ext_ref[tm:tm + CONV_HALO, :]


def _conv(x, gain, w_in, b_in, dw8, dw_b, ln_g, ln_b, w_out, b_out, tm):
    b, t, d = x.shape
    tok = pl.BlockSpec((1, tm, d), lambda i, j: (i, j, 0))
    consts = [gain, w_in, b_in, dw8, dw_b, ln_g, ln_b, w_out, b_out]
    return pl.pallas_call(
        _conv_kernel,
        grid=(b, t // tm),
        in_specs=[tok] + [_const_spec(c.shape) for c in consts],
        out_specs=tok,
        out_shape=jax.ShapeDtypeStruct((b, t, d), F32),
        scratch_shapes=[pltpu.VMEM((tm + CONV_HALO, d), F32),
                        pltpu.VMEM((SUBLANES, tm + CONV_HALO - SUBLANES, d), F32)],
        compiler_params=pltpu.CompilerParams(dimension_semantics=("arbitrary", "arbitrary"),
                                             vmem_limit_bytes=VMEM_LIMIT),
        name="conformer_conv",
    )(x, *consts)


def _row(v):
    return v.reshape(1, -1)


def kernel(x, norm_gains, ffn_w_gate, ffn_w_up, ffn_w_down, rwkv_mu, rwkv_w_rkv, rwkv_w0, rwkv_w1, rwkv_w2, rwkv_a0, rwkv_a1, rwkv_a2, rwkv_g1, rwkv_g2, rwkv_k_k, rwkv_k_a, rwkv_r_k, rwkv_ln_gain, rwkv_ln_bias, rwkv_w_out, conv_w_in, conv_b_in, conv_dw, conv_dw_b, conv_ln_gain, conv_ln_bias, conv_w_out, conv_b_out, final_norm):
    bsz, t, d = x.shape
    depth = norm_gains.shape[0]
    m = bsz * t
    tm_ffn = min(512, m)
    tm_tok = min(256, t)
    tm_proj = min(512, t)
    assert depth >= 1 and d % GROUP_LANES == 0 and m % tm_ffn == 0
    assert t % SCAN_ROWS == 0 and t % tm_tok == 0 and t % tm_proj == 0

    bf = lambda w: w.astype(BF16)
    blk = jnp.arange(GROUP_LANES, dtype=jnp.int32) // HEAD
    ones_bd = (blk[:, None] == blk[None, :]).astype(BF16)
    idx = jnp.arange(SCAN_BLOCK, dtype=jnp.int32)
    tri = ((idx[:, None] >= idx[None, :]) & (idx[:, None] // CHUNK == idx[None, :] // CHUNK)).astype(BF16)

    def ffn(xc, i, s):
        last = i == depth - 1 and s == 1
        out = _ffn(xc.reshape(m, d), _row(norm_gains[i, 2 * s]), bf(ffn_w_gate[i, s]), bf(ffn_w_up[i, s]),
                   bf(ffn_w_down[i, s]), tm_ffn, out_gain=_row(final_norm) if last else None)
        return out.reshape(bsz, t, d)

    for i in range(depth):
        x = ffn(x, i, 0)
        j = i // 2
        gain = _row(norm_gains[i, 1])
        if i % 2 == 0:
            r, lw, k, v, a, b, g = _rwkv_proj(
                x, gain, bf(rwkv_mu[j]), bf(rwkv_w_rkv[j, 0]), bf(rwkv_w_rkv[j, 1]), bf(rwkv_w_rkv[j, 2]),
                _row(rwkv_w0[j]), bf(rwkv_w1[j]), bf(rwkv_w2[j]), _row(rwkv_a0[j]), bf(rwkv_a1[j]),
                bf(rwkv_a2[j]), bf(rwkv_g1[j]), bf(rwkv_g2[j]), _row(rwkv_k_k[j]), _row(rwkv_k_a[j]),
                ones_bd, tm_proj)
            x = _rwkv_mix(x, r, lw, k, v, a, b, g, tri, _row(rwkv_r_k[j]), _row(rwkv_ln_gain[j]),
                          _row(rwkv_ln_bias[j]), bf(rwkv_w_out[j]), ones_bd)
        else:
            dw8 = jnp.repeat(conv_dw[j], SUBLANES, axis=0)
            x = _conv(x, gain, bf(conv_w_in[j]), _row(conv_b_in[j]), dw8, _row(conv_dw_b[j]),
                      _row(conv_ln_gain[j]), _row(conv_ln_bias[j]), bf(conv_w_out[j]), _row(conv_b_out[j]),
                      tm_tok)
        x = ffn(x, i, 1)
    return x
```

```python
import jax
import jax.numpy as jnp
from jax import lax
from jax.experimental import pallas as pl
from jax.experimental.pallas import tpu as pltpu

F32 = jnp.float32
BF16 = jnp.bfloat16

HEAD = 64
GROUP_LANES = 256
HEADS_PER_GROUP = GROUP_LANES // HEAD
CHUNK = 64
SCAN_BLOCK = 2 * CHUNK
SCAN_ROWS = 2 * SCAN_BLOCK
FFN_SUB = 256
CONV_WIDTH = 31
CONV_HALO = 32
CONV_SUB = 128
CONV_ROWS = 32
CONV_LANES = 256
SUBLANES = 8
RMS_EPS = 1e-6
LN_EPS = 1e-5
GN_EPS = 64e-5
L2_EPS = 1e-12
DECAY_SCALE = 0.6065306597126334
VMEM_LIMIT = 56 * 1024 * 1024


def _rms(x, gain):
    ms = jnp.mean(x * x, axis=-1, keepdims=True)
    return x * lax.rsqrt(ms + RMS_EPS) * gain


def _dot(a, b):
    return jnp.dot(a, b, preferred_element_type=F32)


def _dot_nt(a, b):
    return lax.dot_general(a, b, (((1,), (1,)), ((), ())), preferred_element_type=F32)


def _dot_tn(a, b):
    return lax.dot_general(a, b, (((0,), (0,)), ((), ())), preferred_element_type=F32)


def _split2(x):
    hi = x.astype(BF16)
    lo = (x - hi.astype(F32)).astype(BF16)
    return hi, lo


def _split3(x):
    hi = x.astype(BF16)
    r1 = x - hi.astype(F32)
    mid = r1.astype(BF16)
    lo = (r1 - mid.astype(F32)).astype(BF16)
    return hi, mid, lo


def _head_sums(xs, ones_bd):
    outs = []
    for x in xs:
        hi, lo = _split2(x)
        slabs = [slice(s * GROUP_LANES, (s + 1) * GROUP_LANES) for s in range(x.shape[-1] // GROUP_LANES)]
        outs.append(jnp.concatenate([_dot(hi[:, sl], ones_bd) + _dot(lo[:, sl], ones_bd) for sl in slabs],
                                    axis=-1))
    return outs


def _const_spec(shape):
    nd = len(shape)
    return pl.BlockSpec(shape, lambda *_: (0,) * nd, pipeline_mode=pl.Buffered(1))


def _ffn_kernel(x_ref, g_ref, wg_ref, wu_ref, wd_ref, *rest):
    out_gain_ref, o_ref = rest if len(rest) == 2 else (None, rest[0])
    tm = o_ref.shape[0]
    sub = min(FFN_SUB, tm)
    rows = lambda s: slice(s * sub, (s + 1) * sub)
    normed = lambda s: _rms(x_ref[rows(s), :], g_ref[...]).astype(BF16)
    h_next = normed(0)
    for s in range(tm // sub):
        h = h_next
        if (s + 1) * sub < tm:
            h_next = normed(s + 1)
        gate = _dot(h, wg_ref[...])
        up = _dot(h, wu_ref[...])
        act = (gate * jax.nn.sigmoid(gate) * up).astype(BF16)
        y = x_ref[rows(s), :] + 0.5 * _dot(act, wd_ref[...])
        o_ref[rows(s), :] = y if out_gain_ref is None else _rms(y, out_gain_ref[...])


def _ffn(x2, gain, wg, wu, wd, tm, out_gain=None):
    m, d = x2.shape
    consts = [gain, wg, wu, wd] + ([] if out_gain is None else [out_gain])
    return pl.pallas_call(
        _ffn_kernel,
        grid=(m // tm,),
        in_specs=[pl.BlockSpec((tm, d), lambda i: (i, 0))] + [_const_spec(c.shape) for c in consts],
        out_specs=pl.BlockSpec((tm, d), lambda i: (i, 0)),
        out_shape=jax.ShapeDtypeStruct((m, d), F32),
        compiler_params=pltpu.CompilerParams(dimension_semantics=("arbitrary",),
                                             vmem_limit_bytes=VMEM_LIMIT),
        name="ffn",
    )(x2, *consts)


def _rwkv_proj_kernel(x_ref, gain_ref, mu_ref, wr_ref, wk_ref, wv_ref, w0_ref, w1_ref, w2_ref,
                      a0_ref, a1_ref, a2_ref, g1_ref, g2_ref, kk_ref, ka_ref, ones_ref,
                      r_out, lw_out, k_out, v_out, a_out, b_out, g_out, carry_ref):
    t = pl.program_id(1)

    @pl.when(t == 0)
    def _():
        carry_ref[...] = jnp.zeros_like(carry_ref)

    h = _rms(x_ref[0], gain_ref[...])
    tm = h.shape[0]
    row = lax.broadcasted_iota(jnp.int32, h.shape, 0)
    h_prev = jnp.where(row == 0, carry_ref[0:1, :], pltpu.roll(h, 1, axis=0))
    carry_ref[0:1, :] = h[tm - 1:tm, :]
    hb = h.astype(BF16)
    xxb = (h_prev - h).astype(BF16)
    mix = lambda i: hb + xxb * mu_ref[i:i + 1, :]

    r = _dot(mix(0), wr_ref[...])
    k = _dot(mix(2), wk_ref[...])
    v = _dot(mix(3), wv_ref[...])
    zw = w0_ref[...] + _dot(jnp.tanh(_dot(mix(1), w1_ref[...])).astype(BF16), w2_ref[...])
    lw = -DECAY_SCALE * jax.nn.sigmoid(zw)
    a = jax.nn.sigmoid(a0_ref[...] + _dot(_dot(mix(4), a1_ref[...]).astype(BF16), a2_ref[...]))
    g = _dot(jax.nn.sigmoid(_dot(mix(5), g1_ref[...])).astype(BF16), g2_ref[...])

    kk = k * kk_ref[...]
    norm = jnp.sqrt(_head_sums([kk * kk], ones_ref[...])[0])
    kk = kk / jnp.maximum(norm, L2_EPS)

    r_out[0] = r.astype(BF16)
    lw_out[0] = lw
    k_out[0] = (k * (1.0 + (a - 1.0) * ka_ref[...])).astype(BF16)
    v_out[0] = v.astype(BF16)
    a_out[0] = (-kk).astype(BF16)
    b_out[0] = (kk * a).astype(BF16)
    g_out[0] = g.astype(BF16)


def _rwkv_proj(x, gain, mu, wr, wk, wv, w0, w1, w2, a0, a1, a2, g1, g2, k_k, k_a, ones_bd, tm):
    b, t, d = x.shape
    tok = pl.BlockSpec((1, tm, d), lambda i, j: (i, j, 0))
    consts = [gain, mu, wr, wk, wv, w0, w1, w2, a0, a1, a2, g1, g2, k_k, k_a, ones_bd]
    out_dtypes = [BF16, F32, BF16, BF16, BF16, BF16, BF16]
    return pl.pallas_call(
        _rwkv_proj_kernel,
        grid=(b, t // tm),
        in_specs=[tok] + [_const_spec(c.shape) for c in consts],
        out_specs=[tok] * 7,
        out_shape=[jax.ShapeDtypeStruct((b, t, d), dt) for dt in out_dtypes],
        scratch_shapes=[pltpu.VMEM((SUBLANES, d), F32)],
        compiler_params=pltpu.CompilerParams(dimension_semantics=("arbitrary", "arbitrary"),
                                             vmem_limit_bytes=VMEM_LIMIT),
        name="rwkv_proj",
    )(x, *consts)


def _stack_heads(x, blk):
    return jnp.concatenate(
        [jnp.where(blk == h, x, 0.0).astype(BF16) for h in range(HEADS_PER_GROUP)], axis=0)


_SCALED_NAMES = ("r_t", "a_t", "k_t", "b_t", "k_h", "b_h", "g_end", "vv")


def _scan_scale_tasks(load, tri, scaled):
    c, gl = CHUNK, GROUP_LANES
    rows = tri.shape[0]
    for name in _SCALED_NAMES:
        scaled[name] = []
    cs_box = []

    def cumulative_decay():
        hi, mid, lo = _split3(load("lw", slice(0, rows)))
        cs_box.append(_dot(tri, hi) + _dot(tri, mid) + _dot(tri, lo))

    def scale_chunk(ci):
        rs = slice(ci * c, (ci + 1) * c)
        cs_c = cs_box[0][rs]
        cs_end = cs_c[c - 1:c, :]
        e_neg = jnp.exp(-cs_c)
        e_end = jnp.exp(cs_end - cs_c)
        k, b = load("k", rs), load("b", rs)
        full = dict(r_t=load("r", rs) * jnp.exp(cs_c), a_t=load("a", rs) * jnp.exp(cs_c - load("lw", rs)),
                    k_t=k * e_neg, b_t=b * e_neg, k_h=k * e_end, b_h=b * e_end,
                    g_end=jnp.exp(cs_end), vv=load("v", rs))
        for g in range(cs_c.shape[1] // gl):
            for name in _SCALED_NAMES:
                scaled[name].append(full[name][:, g * gl:(g + 1) * gl])

    return [cumulative_decay] + [lambda ci=ci: scale_chunk(ci) for ci in range(rows // c)]


def _scan_local(scaled, hook):
    r_t, a_t, k_t, b_t, k_h, b_h, g_end, vv = (scaled[name] for name in _SCALED_NAMES)
    c, gl = CHUNK, GROUP_LANES
    assert HEADS_PER_GROUP * c == gl
    nu = range(len(r_t))

    rowi = lax.broadcasted_iota(jnp.int32, (c, gl), 0)
    lane = lax.broadcasted_iota(jnp.int32, (c, gl), 1)
    blk = lane // HEAD
    col = lane - blk * HEAD
    strict = col < rowi
    incl = col <= rowi
    eye = jnp.where(col == rowi, 1.0, 0.0)
    stack = lambda x: _stack_heads(x, blk)

    aa = [_dot_nt(jnp.concatenate([a_t[u], r_t[u]], axis=0).astype(BF16),
                  jnp.concatenate([stack(b_t[u]), stack(k_t[u])], axis=0)) for u in nu]
    p = [jnp.where(strict, aa[u][:c, :gl], 0.0) for u in nu]
    a_xk = [jnp.concatenate([jnp.where(strict, aa[u][:c, gl:], 0.0),
                             jnp.where(incl, aa[u][c:, gl:], 0.0)], axis=0).astype(BF16) for u in nu]
    a_rb = [jnp.where(incl, aa[u][c:, :gl], 0.0) for u in nu]
    hook()

    xkv = [_dot(a_xk[u], stack(vv[u])) for u in nu]
    hook()

    n_sq = c.bit_length() - 1
    assert n_sq >= 2
    both = [_dot(jnp.concatenate([p[u], a_rb[u]], axis=0).astype(BF16), stack(p[u])) for u in nu]
    tq = [jnp.concatenate([eye + p[u], a_rb[u] + both[u][c:]], axis=0) for u in nu]
    p = [both[u][:c] for u in nu]
    hook()
    for i in range(1, n_sq):
        if i < n_sq - 1:
            both = [_dot(jnp.concatenate([p[u], tq[u]], axis=0).astype(BF16), stack(p[u])) for u in nu]
            p = [both[u][:c] for u in nu]
            tq = [tq[u] + both[u][c:] for u in nu]
        else:
            tq = [tq[u] + _dot(tq[u].astype(BF16), stack(p[u])) for u in nu]
        hook()

    tx = [_dot(tq[u].astype(BF16), jnp.concatenate([stack(a_t[u]), stack(xkv[u][:c])], axis=1)) for u in nu]
    hook()
    return dict(
        ra_hat=[jnp.concatenate([r_t[u] + tx[u][c:, :gl], tx[u][:c, :gl]], axis=0).astype(BF16) for u in nu],
        y0=[tx[u][c:, gl:] + xkv[u][c:] for u in nu],
        u0=[tx[u][:c, gl:] for u in nu],
        bk_h=[jnp.concatenate([b_h[u], k_h[u]], axis=0).astype(BF16) for u in nu],
        vv=vv, g_end=g_end)


def _scan_state_tasks(loc, states, ys):
    c, gl = CHUNK, GROUP_LANES
    n_groups = len(states)
    srow = lax.broadcasted_iota(jnp.int32, (gl, gl), 0) // HEAD
    scol = lax.broadcasted_iota(jnp.int32, (gl, gl), 1) // HEAD
    on_diag = srow == scol
    tasks = []
    for ci in range(len(loc["y0"]) // n_groups):
        us = [ci * n_groups + g for g in range(n_groups)]
        nt = []

        def read_state(us=us, nt=nt):
            nt[:] = [_dot_nt(loc["ra_hat"][u], states[g].astype(BF16)) for g, u in enumerate(us)]
            ys.append(jnp.concatenate([nt[g][:c] + loc["y0"][u] for g, u in enumerate(us)], axis=-1))

        def advance_state(us=us, nt=nt):
            uv = [jnp.concatenate([nt[g][c:] + loc["u0"][u], loc["vv"][u]], axis=0).astype(BF16)
                  for g, u in enumerate(us)]
            states[:] = [jnp.where(on_diag, states[g] * loc["g_end"][u] + _dot_tn(uv[g], loc["bk_h"][u]), 0.0)
                         for g, u in enumerate(us)]

        tasks += [read_state, advance_state]
    return tasks


def _rwkv_mix_kernel(x_ref, r_ref, lw_ref, k_ref, v_ref, a_ref, b_ref, g_ref, tri_ref, rk_ref, lng_ref,
                     lnb_ref, wo_ref, ones_ref, o_ref, s_ref):
    @pl.when(pl.program_id(1) == 0)
    def _():
        s_ref[...] = jnp.zeros_like(s_ref)

    n_groups = s_ref.shape[0]
    n_blocks = o_ref.shape[1] // SCAN_BLOCK
    f32_rows = lambda ref, rs: ref[0, rs, :].astype(F32)
    token_refs = dict(r=r_ref, lw=lw_ref, k=k_ref, v=v_ref, a=a_ref, b=b_ref)
    states = [s_ref[g] for g in range(n_groups)]
    pending = []

    def hook():
        if pending:
            pending.pop(0)()

    def block_loader(blk_i):
        base = blk_i * SCAN_BLOCK
        return lambda name, rs: f32_rows(token_refs[name], slice(base + rs.start, base + rs.stop))

    scaled = [dict() for _ in range(n_blocks)]
    scale_tasks = [_scan_scale_tasks(block_loader(i), tri_ref[...], scaled[i]) for i in range(n_blocks)]
    ys = [[] for _ in range(n_blocks)]
    for task in scale_tasks[0]:
        task()
    for blk_i in range(n_blocks):
        next_scale = scale_tasks[blk_i + 1] if blk_i + 1 < n_blocks else []
        pending += next_scale
        loc = _scan_local(scaled[blk_i], hook)
        while any(task in pending for task in next_scale):
            pending.pop(0)()
        pending += _scan_state_tasks(loc, states, ys[blk_i])

    ones_bd = ones_ref[...]
    for blk_i in range(n_blocks):
        rs = slice(blk_i * SCAN_BLOCK, (blk_i + 1) * SCAN_BLOCK)
        while len(ys[blk_i]) < SCAN_BLOCK // CHUNK:
            pending.pop(0)()
        y = jnp.concatenate(ys[blk_i], axis=0)
        r, k, v = f32_rows(r_ref, rs), f32_rows(k_ref, rs), f32_rows(v_ref, rs)
        y_sum, rk_sum = _head_sums([y, r * k * rk_ref[...]], ones_bd)
        hook()
        dlt = y - y_sum * (1.0 / HEAD)
        var = _head_sums([dlt * dlt], ones_bd)[0] * (1.0 / HEAD)
        hook()
        yn = dlt * lax.rsqrt(var + GN_EPS) * lng_ref[...] + lnb_ref[...]
        out = ((yn + rk_sum * v) * f32_rows(g_ref, rs)).astype(BF16)
        hook()
        o_ref[0, rs, :] = x_ref[0, rs, :] + _dot(out, wo_ref[...])
        hook()
    while pending:
        pending.pop(0)()
    for g in range(n_groups):
        s_ref[g] = states[g]


def _rwkv_mix(x, r, lw, k, v, a, b, g, tri, r_k, ln_g, ln_b, wo, ones_bd):
    bsz, t, d = x.shape
    tok = pl.BlockSpec((1, SCAN_ROWS, d), lambda i, j: (i, j, 0))
    consts = [tri, r_k, ln_g, ln_b, wo, ones_bd]
    return pl.pallas_call(
        _rwkv_mix_kernel,
        grid=(bsz, t // SCAN_ROWS),
        in_specs=[tok] * 8 + [_const_spec(c.shape) for c in consts],
        out_specs=tok,
        out_shape=jax.ShapeDtypeStruct((bsz, t, d), F32),
        scratch_shapes=[pltpu.VMEM((d // GROUP_LANES, GROUP_LANES, GROUP_LANES), F32)],
        compiler_params=pltpu.CompilerParams(dimension_semantics=("arbitrary", "arbitrary"),
                                             vmem_limit_bytes=VMEM_LIMIT),
        name="rwkv_mix",
    )(x, r, lw, k, v, a, b, g, *consts)


def _conv_kernel(x_ref, gain_ref, win_ref, bin_ref, dw_ref, dwb_ref, lng_ref, lnb_ref, wout_ref, bout_ref,
                 o_ref, ext_ref, sh_ref):
    tm, d = o_ref.shape[1], o_ref.shape[2]
    sub = min(CONV_SUB, tm)
    lanes = min(CONV_LANES, d)
    off = CONV_HALO - (CONV_WIDTH - 1)
    span = sub + CONV_HALO
    nrb = CONV_ROWS // SUBLANES

    @pl.when(pl.program_id(1) == 0)
    def _():
        ext_ref[0:CONV_HALO, :] = jnp.zeros((CONV_HALO, d), F32)

    for st in range(tm // sub):
        r0 = st * sub
        x = x_ref[0, r0:r0 + sub, :]
        h = _rms(x, gain_ref[...]).astype(BF16)
        z = _dot(h, win_ref[...]) + bin_ref[...]
        ext_ref[CONV_HALO + r0:CONV_HALO + r0 + sub, :] = z[:, :d] * jax.nn.sigmoid(z[:, d:])
        window = ext_ref[r0:r0 + span, :]
        for s in range(1, SUBLANES):
            sh_ref[s, r0:r0 + span - SUBLANES, :] = pltpu.roll(window, span - s, axis=0)[:span - SUBLANES]

        blocks = []
        for rb in range(sub // CONV_ROWS):
            lane_blocks = []
            for lb in range(d // lanes):
                ls = slice(lb * lanes, (lb + 1) * lanes)
                acc = jnp.broadcast_to(dwb_ref[:, ls], (CONV_ROWS, lanes)).reshape(nrb, SUBLANES, lanes)
                for j in range(CONV_WIDTH):
                    q, s = divmod(off + j, SUBLANES)
                    lo = r0 + rb * CONV_ROWS + q * SUBLANES
                    win = ext_ref[lo:lo + CONV_ROWS, ls] if s == 0 else sh_ref[s, lo:lo + CONV_ROWS, ls]
                    tap = dw_ref[j * SUBLANES:(j + 1) * SUBLANES, ls]
                    acc = acc + tap[None] * win.reshape(nrb, SUBLANES, lanes)
                lane_blocks.append(acc.reshape(CONV_ROWS, lanes))
            blocks.append(jnp.concatenate(lane_blocks, axis=-1))
        cv = jnp.concatenate(blocks, axis=0)

        mean = jnp.mean(cv, axis=-1, keepdims=True)
        dlt = cv - mean
        var = jnp.mean(dlt * dlt, axis=-1, keepdims=True)
        hn = dlt * lax.rsqrt(var + LN_EPS) * lng_ref[...] + lnb_ref[...]
        act = (hn * jax.nn.sigmoid(hn)).astype(BF16)
        o_ref[0, r0:r0 + sub, :] = x + _dot(act, wout_ref[...]) + bout_ref[...]

    ext_ref[0:CONV_HALO, :] = ext_ref[tm:tm + CONV_HALO, :]


def _conv(x, gain, w_in, b_in, dw8, dw_b, ln_g, ln_b, w_out, b_out, tm):
    b, t, d = x.shape
    tok = pl.BlockSpec((1, tm, d), lambda i, j: (i, j, 0))
    consts = [gain, w_in, b_in, dw8, dw_b, ln_g, ln_b, w_out, b_out]
    return pl.pallas_call(
        _conv_kernel,
        grid=(b, t // tm),
        in_specs=[tok] + [_const_spec(c.shape) for c in consts],
        out_specs=tok,
        out_shape=jax.ShapeDtypeStruct((b, t, d), F32),
        scratch_shapes=[pltpu.VMEM((tm + CONV_HALO, d), F32),
                        pltpu.VMEM((SUBLANES, tm + CONV_HALO - SUBLANES, d), F32)],
        compiler_params=pltpu.CompilerParams(dimension_semantics=("arbitrary", "arbitrary"),
                                             vmem_limit_bytes=VMEM_LIMIT),
        name="conformer_conv",
    )(x, *consts)


def _row(v):
    return v.reshape(1, -1)


def kernel(x, norm_gains, ffn_w_gate, ffn_w_up, ffn_w_down, rwkv_mu, rwkv_w_rkv, rwkv_w0, rwkv_w1, rwkv_w2, rwkv_a0, rwkv_a1, rwkv_a2, rwkv_g1, rwkv_g2, rwkv_k_k, rwkv_k_a, rwkv_r_k, rwkv_ln_gain, rwkv_ln_bias, rwkv_w_out, conv_w_in, conv_b_in, conv_dw, conv_dw_b, conv_ln_gain, conv_ln_bias, conv_w_out, conv_b_out, final_norm):
    bsz, t, d = x.shape
    depth = norm_gains.shape[0]
    m = bsz * t
    tm_ffn = min(1024, m)
    tm_tok = min(256, t)
    tm_proj = min(512, t)
    assert depth >= 1 and d % GROUP_LANES == 0 and m % tm_ffn == 0
    assert t % SCAN_ROWS == 0 and t % tm_tok == 0 and t % tm_proj == 0

    bf = lambda w: w.astype(BF16)
    blk = jnp.arange(GROUP_LANES, dtype=jnp.int32) // HEAD
    ones_bd = (blk[:, None] == blk[None, :]).astype(BF16)
    idx = jnp.arange(SCAN_BLOCK, dtype=jnp.int32)
    tri = ((idx[:, None] >= idx[None, :]) & (idx[:, None] // CHUNK == idx[None, :] // CHUNK)).astype(BF16)

    def ffn(xc, i, s):
        last = i == depth - 1 and s == 1
        out = _ffn(xc.reshape(m, d), _row(norm_gains[i, 2 * s]), bf(ffn_w_gate[i, s]), bf(ffn_w_up[i, s]),
                   bf(ffn_w_down[i, s]), tm_ffn, out_gain=_row(final_norm) if last else None)
        return out.reshape(bsz, t, d)

    for i in range(depth):
        x = ffn(x, i, 0)
        j = i // 2
        gain = _row(norm_gains[i, 1])
        if i % 2 == 0:
            r, lw, k, v, a, b, g = _rwkv_proj(
                x, gain, bf(rwkv_mu[j]), bf(rwkv_w_rkv[j, 0]), bf(rwkv_w_rkv[j, 1]), bf(rwkv_w_rkv[j, 2]),
                _row(rwkv_w0[j]), bf(rwkv_w1[j]), bf(rwkv_w2[j]), _row(rwkv_a0[j]), bf(rwkv_a1[j]),
                bf(rwkv_a2[j]), bf(rwkv_g1[j]), bf(rwkv_g2[j]), _row(rwkv_k_k[j]), _row(rwkv_k_a[j]),
                ones_bd, tm_proj)
            x = _rwkv_mix(x, r, lw, k, v, a, b, g, tri, _row(rwkv_r_k[j]), _row(rwkv_ln_gain[j]),
                          _row(rwkv_ln_bias[j]), bf(rwkv_w_out[j]), ones_bd)
        else:
            dw8 = jnp.repeat(conv_dw[j], SUBLANES, axis=0)
            x = _conv(x, gain, bf(conv_w_in[j]), _row(conv_b_in[j]), dw8, _row(conv_dw_b[j]),
                      _row(conv_ln_gain[j]), _row(conv_ln_bias[j]), bf(conv_w_out[j]), _row(conv_b_out[j]),
                      tm_tok)
        x = ffn(x, i, 1)
    return x
```

```python
import jax
import jax.numpy as jnp
from jax import lax
from jax.experimental import pallas as pl
from jax.experimental.pallas import tpu as pltpu

F32 = jnp.float32
BF16 = jnp.bfloat16

HEAD = 64
GROUP_LANES = 256
HEADS_PER_GROUP = GROUP_LANES // HEAD
CHUNK = 64
SCAN_BLOCK = 2 * CHUNK
SCAN_ROWS = 4 * SCAN_BLOCK
FFN_SUB = 256
CONV_WIDTH = 31
CONV_HALO = 32
CONV_SUB = 128
CONV_ROWS = 32
CONV_LANES = 256
SUBLANES = 8
TOKEN_FIELDS = ("r", "k", "v", "a", "b", "g")
RMS_EPS = 1e-6
LN_EPS = 1e-5
GN_EPS = 64e-5
L2_EPS = 1e-12
DECAY_SCALE = 0.6065306597126334
VMEM_LIMIT = 56 * 1024 * 1024


def _rms(x, gain):
    ms = jnp.mean(x * x, axis=-1, keepdims=True)
    return x * lax.rsqrt(ms + RMS_EPS) * gain


def _dot(a, b):
    return jnp.dot(a, b, preferred_element_type=F32)


def _dot_nt(a, b):
    return lax.dot_general(a, b, (((1,), (1,)), ((), ())), preferred_element_type=F32)


def _dot_tn(a, b):
    return lax.dot_general(a, b, (((0,), (0,)), ((), ())), preferred_element_type=F32)


def _split2(x):
    hi = x.astype(BF16)
    lo = (x - hi.astype(F32)).astype(BF16)
    return hi, lo


def _split3(x):
    hi = x.astype(BF16)
    r1 = x - hi.astype(F32)
    mid = r1.astype(BF16)
    lo = (r1 - mid.astype(F32)).astype(BF16)
    return hi, mid, lo


def _head_sums(xs, ones_bd):
    outs = []
    for x in xs:
        hi, lo = _split2(x)
        slabs = [slice(s * GROUP_LANES, (s + 1) * GROUP_LANES) for s in range(x.shape[-1] // GROUP_LANES)]
        outs.append(jnp.concatenate([_dot(hi[:, sl], ones_bd) + _dot(lo[:, sl], ones_bd) for sl in slabs],
                                    axis=-1))
    return outs


def _const_spec(shape):
    nd = len(shape)
    return pl.BlockSpec(shape, lambda *_: (0,) * nd, pipeline_mode=pl.Buffered(1))


def _ffn_kernel(x_ref, g_ref, wg_ref, wu_ref, wd_ref, *rest):
    out_gain_ref, o_ref = rest if len(rest) == 2 else (None, rest[0])
    tm = o_ref.shape[0]
    sub = min(FFN_SUB, tm)
    rows = lambda s: slice(s * sub, (s + 1) * sub)
    normed = lambda s: _rms(x_ref[rows(s), :], g_ref[...]).astype(BF16)

    def finish(s, down):
        y = x_ref[rows(s), :] + 0.5 * down
        o_ref[rows(s), :] = y if out_gain_ref is None else _rms(y, out_gain_ref[...])

    h_next = normed(0)
    finish_prev = None
    for s in range(tm // sub):
        h = h_next
        if (s + 1) * sub < tm:
            h_next = normed(s + 1)
        gate = _dot(h, wg_ref[...])
        up = _dot(h, wu_ref[...])
        if finish_prev is not None:
            finish_prev()
        act = (gate * jax.nn.sigmoid(gate) * up).astype(BF16)
        down = _dot(act, wd_ref[...])
        finish_prev = lambda s=s, down=down: finish(s, down)
    finish_prev()


def _ffn(x2, gain, wg, wu, wd, tm, out_gain=None):
    m, d = x2.shape
    consts = [gain, wg, wu, wd] + ([] if out_gain is None else [out_gain])
    return pl.pallas_call(
        _ffn_kernel,
        grid=(m // tm,),
        in_specs=[pl.BlockSpec((tm, d), lambda i: (i, 0))] + [_const_spec(c.shape) for c in consts],
        out_specs=pl.BlockSpec((tm, d), lambda i: (i, 0)),
        out_shape=jax.ShapeDtypeStruct((m, d), F32),
        compiler_params=pltpu.CompilerParams(dimension_semantics=("arbitrary",),
                                             vmem_limit_bytes=VMEM_LIMIT),
        name="ffn",
    )(x2, *consts)


def _rwkv_proj_kernel(x_ref, gain_ref, mu_ref, wr_ref, wk_ref, wv_ref, w0_ref, w1_ref, w2_ref,
                      a0_ref, a1_ref, a2_ref, g1_ref, g2_ref, kk_ref, ka_ref, ones_ref,
                      tok_out, lw_out, carry_ref):
    t = pl.program_id(1)

    @pl.when(t == 0)
    def _():
        carry_ref[...] = jnp.zeros_like(carry_ref)

    h = _rms(x_ref[0], gain_ref[...])
    tm = h.shape[0]
    row = lax.broadcasted_iota(jnp.int32, h.shape, 0)
    h_prev = jnp.where(row == 0, carry_ref[0:1, :], pltpu.roll(h, 1, axis=0))
    carry_ref[0:1, :] = h[tm - 1:tm, :]
    hb = h.astype(BF16)
    xxb = (h_prev - h).astype(BF16)
    mix = lambda i: hb + xxb * mu_ref[i:i + 1, :]

    r = _dot(mix(0), wr_ref[...])
    k = _dot(mix(2), wk_ref[...])
    v = _dot(mix(3), wv_ref[...])
    zw = w0_ref[...] + _dot(jnp.tanh(_dot(mix(1), w1_ref[...])).astype(BF16), w2_ref[...])
    lw = -DECAY_SCALE * jax.nn.sigmoid(zw)
    a = jax.nn.sigmoid(a0_ref[...] + _dot(_dot(mix(4), a1_ref[...]).astype(BF16), a2_ref[...]))
    g = _dot(jax.nn.sigmoid(_dot(mix(5), g1_ref[...])).astype(BF16), g2_ref[...])

    kk = k * kk_ref[...]
    norm = jnp.sqrt(_head_sums([kk * kk], ones_ref[...])[0])
    kk = kk / jnp.maximum(norm, L2_EPS)

    lw_out[0] = lw
    d = lw.shape[-1]
    fields = dict(r=r, k=k * (1.0 + (a - 1.0) * ka_ref[...]), v=v, a=-kk, b=kk * a, g=g)
    for i, name in enumerate(TOKEN_FIELDS):
        tok_out[0, :, i * d:(i + 1) * d] = fields[name].astype(BF16)


def _rwkv_proj(x, gain, mu, wr, wk, wv, w0, w1, w2, a0, a1, a2, g1, g2, k_k, k_a, ones_bd, tm):
    b, t, d = x.shape
    tok = pl.BlockSpec((1, tm, d), lambda i, j: (i, j, 0))
    consts = [gain, mu, wr, wk, wv, w0, w1, w2, a0, a1, a2, g1, g2, k_k, k_a, ones_bd]
    nf = len(TOKEN_FIELDS)
    return pl.pallas_call(
        _rwkv_proj_kernel,
        grid=(b, t // tm),
        in_specs=[tok] + [_const_spec(c.shape) for c in consts],
        out_specs=[pl.BlockSpec((1, tm, nf * d), lambda i, j: (i, j, 0)), tok],
        out_shape=[jax.ShapeDtypeStruct((b, t, nf * d), BF16), jax.ShapeDtypeStruct((b, t, d), F32)],
        scratch_shapes=[pltpu.VMEM((SUBLANES, d), F32)],
        compiler_params=pltpu.CompilerParams(dimension_semantics=("arbitrary", "arbitrary"),
                                             vmem_limit_bytes=VMEM_LIMIT),
        name="rwkv_proj",
    )(x, *consts)


def _stack_heads(x, blk):
    return jnp.concatenate(
        [jnp.where(blk == h, x, 0.0).astype(BF16) for h in range(HEADS_PER_GROUP)], axis=0)


_SCALED_NAMES = ("r_t", "a_t", "k_t", "b_t", "k_h", "b_h", "g_end", "vv")


def _scan_scale_tasks(load, tri, scaled):
    c, gl = CHUNK, GROUP_LANES
    rows = tri.shape[0]
    for name in _SCALED_NAMES:
        scaled[name] = []
    cs_box = []

    def cumulative_decay():
        hi, mid, lo = _split3(load("lw", slice(0, rows)))
        cs_box.append(_dot(tri, hi) + _dot(tri, mid) + _dot(tri, lo))

    def scale_chunk(ci):
        rs = slice(ci * c, (ci + 1) * c)
        cs_c = cs_box[0][rs]
        cs_end = cs_c[c - 1:c, :]
        e_neg = jnp.exp(-cs_c)
        e_end = jnp.exp(cs_end - cs_c)
        k, b = load("k", rs), load("b", rs)
        full = dict(r_t=load("r", rs) * jnp.exp(cs_c), a_t=load("a", rs) * jnp.exp(cs_c - load("lw", rs)),
                    k_t=k * e_neg, b_t=b * e_neg, k_h=k * e_end, b_h=b * e_end,
                    g_end=jnp.exp(cs_end), vv=load("v", rs))
        for g in range(cs_c.shape[1] // gl):
            for name in _SCALED_NAMES:
                scaled[name].append(full[name][:, g * gl:(g + 1) * gl])

    return [cumulative_decay] + [lambda ci=ci: scale_chunk(ci) for ci in range(rows // c)]


def _scan_local(scaled, hook):
    r_t, a_t, k_t, b_t, k_h, b_h, g_end, vv = (scaled[name] for name in _SCALED_NAMES)
    c, gl = CHUNK, GROUP_LANES
    assert HEADS_PER_GROUP * c == gl
    nu = range(len(r_t))

    rowi = lax.broadcasted_iota(jnp.int32, (c, gl), 0)
    lane = lax.broadcasted_iota(jnp.int32, (c, gl), 1)
    blk = lane // HEAD
    col = lane - blk * HEAD
    strict = col < rowi
    incl = col <= rowi
    eye = jnp.where(col == rowi, 1.0, 0.0)
    stack = lambda x: _stack_heads(x, blk)

    aa = [_dot_nt(jnp.concatenate([a_t[u], r_t[u]], axis=0).astype(BF16),
                  jnp.concatenate([stack(b_t[u]), stack(k_t[u])], axis=0)) for u in nu]
    p = [jnp.where(strict, aa[u][:c, :gl], 0.0) for u in nu]
    a_xk = [jnp.concatenate([jnp.where(strict, aa[u][:c, gl:], 0.0),
                             jnp.where(incl, aa[u][c:, gl:], 0.0)], axis=0).astype(BF16) for u in nu]
    a_rb = [jnp.where(incl, aa[u][c:, :gl], 0.0) for u in nu]
    hook()

    xkv = [_dot(a_xk[u], stack(vv[u])) for u in nu]
    hook()

    n_sq = c.bit_length() - 1
    assert n_sq >= 2
    both = [_dot(jnp.concatenate([p[u], a_rb[u]], axis=0).astype(BF16), stack(p[u])) for u in nu]
    tq = [jnp.concatenate([eye + p[u], a_rb[u] + both[u][c:]], axis=0) for u in nu]
    p = [both[u][:c] for u in nu]
    hook()
    for i in range(1, n_sq):
        if i < n_sq - 1:
            both = [_dot(jnp.concatenate([p[u], tq[u]], axis=0).astype(BF16), stack(p[u])) for u in nu]
            p = [both[u][:c] for u in nu]
            tq = [tq[u] + both[u][c:] for u in nu]
        else:
            tq = [tq[u] + _dot(tq[u].astype(BF16), stack(p[u])) for u in nu]
        hook()

    tx = [_dot(tq[u].astype(BF16), jnp.concatenate([stack(a_t[u]), stack(xkv[u][:c])], axis=1)) for u in nu]
    hook()
    return dict(
        ra_hat=[jnp.concatenate([r_t[u] + tx[u][c:, :gl], tx[u][:c, :gl]], axis=0).astype(BF16) for u in nu],
        y0=[tx[u][c:, gl:] + xkv[u][c:] for u in nu],
        u0=[tx[u][:c, gl:] for u in nu],
        bk_h=[jnp.concatenate([b_h[u], k_h[u]], axis=0).astype(BF16) for u in nu],
        vv=vv, g_end=g_end)


def _scan_state_tasks(loc, states, ys):
    c, gl = CHUNK, GROUP_LANES
    n_groups = len(states)
    srow = lax.broadcasted_iota(jnp.int32, (gl, gl), 0) // HEAD
    scol = lax.broadcasted_iota(jnp.int32, (gl, gl), 1) // HEAD
    on_diag = srow == scol
    tasks = []
    for ci in range(len(loc["y0"]) // n_groups):
        us = [ci * n_groups + g for g in range(n_groups)]
        nt = []

        def read_state(us=us, nt=nt):
            nt[:] = [_dot_nt(loc["ra_hat"][u], states[g].astype(BF16)) for g, u in enumerate(us)]
            ys.append(jnp.concatenate([nt[g][:c] + loc["y0"][u] for g, u in enumerate(us)], axis=-1))

        def advance_state(us=us, nt=nt):
            uv = [jnp.concatenate([nt[g][c:] + loc["u0"][u], loc["vv"][u]], axis=0).astype(BF16)
                  for g, u in enumerate(us)]
            states[:] = [jnp.where(on_diag, states[g] * loc["g_end"][u] + _dot_tn(uv[g], loc["bk_h"][u]), 0.0)
                         for g, u in enumerate(us)]

        tasks += [read_state, advance_state]
    return tasks


def _rwkv_mix_kernel(x_ref, tok_ref, lw_ref, tri_ref, rk_ref, lng_ref, lnb_ref, wo_ref, ones_ref, o_ref, s_ref):
    @pl.when(pl.program_id(1) == 0)
    def _():
        s_ref[...] = jnp.zeros_like(s_ref)

    n_groups = s_ref.shape[0]
    n_blocks = o_ref.shape[1] // SCAN_BLOCK
    d = o_ref.shape[2]

    def token_rows(name, rs):
        if name == "lw":
            return lw_ref[0, rs, :]
        i = TOKEN_FIELDS.index(name)
        return tok_ref[0, rs, i * d:(i + 1) * d].astype(F32)

    states = [s_ref[g] for g in range(n_groups)]
    pending = []

    def hook():
        if pending:
            pending.pop(0)()

    def block_loader(blk_i):
        base = blk_i * SCAN_BLOCK
        return lambda name, rs: token_rows(name, slice(base + rs.start, base + rs.stop))

    scaled = [dict() for _ in range(n_blocks)]
    scale_tasks = [_scan_scale_tasks(block_loader(i), tri_ref[...], scaled[i]) for i in range(n_blocks)]
    ys = [[] for _ in range(n_blocks)]
    for task in scale_tasks[0]:
        task()
    for blk_i in range(n_blocks):
        next_scale = scale_tasks[blk_i + 1] if blk_i + 1 < n_blocks else []
        pending += next_scale
        loc = _scan_local(scaled[blk_i], hook)
        while any(task in pending for task in next_scale):
            pending.pop(0)()
        pending += _scan_state_tasks(loc, states, ys[blk_i])

    ones_bd = ones_ref[...]
    for blk_i in range(n_blocks):
        rs = slice(blk_i * SCAN_BLOCK, (blk_i + 1) * SCAN_BLOCK)
        while len(ys[blk_i]) < SCAN_BLOCK // CHUNK:
            pending.pop(0)()
        y = jnp.concatenate(ys[blk_i], axis=0)
        r, k, v = token_rows("r", rs), token_rows("k", rs), token_rows("v", rs)
        y_sum, rk_sum = _head_sums([y, r * k * rk_ref[...]], ones_bd)
        hook()
        dlt = y - y_sum * (1.0 / HEAD)
        var = _head_sums([dlt * dlt], ones_bd)[0] * (1.0 / HEAD)
        hook()
        yn = dlt * lax.rsqrt(var + GN_EPS) * lng_ref[...] + lnb_ref[...]
        out = ((yn + rk_sum * v) * token_rows("g", rs)).astype(BF16)
        hook()
        o_ref[0, rs, :] = x_ref[0, rs, :] + _dot(out, wo_ref[...])
        hook()
    while pending:
        pending.pop(0)()
    for g in range(n_groups):
        s_ref[g] = states[g]


def _rwkv_mix(x, tokens, lw, tri, r_k, ln_g, ln_b, wo, ones_bd):
    bsz, t, d = x.shape
    tok = pl.BlockSpec((1, SCAN_ROWS, d), lambda i, j: (i, j, 0))
    wide = pl.BlockSpec((1, SCAN_ROWS, tokens.shape[2]), lambda i, j: (i, j, 0))
    consts = [tri, r_k, ln_g, ln_b, wo, ones_bd]
    return pl.pallas_call(
        _rwkv_mix_kernel,
        grid=(bsz, t // SCAN_ROWS),
        in_specs=[tok, wide, tok] + [_const_spec(c.shape) for c in consts],
        out_specs=tok,
        out_shape=jax.ShapeDtypeStruct((bsz, t, d), F32),
        scratch_shapes=[pltpu.VMEM((d // GROUP_LANES, GROUP_LANES, GROUP_LANES), F32)],
        compiler_params=pltpu.CompilerParams(dimension_semantics=("arbitrary", "arbitrary"),
                                             vmem_limit_bytes=VMEM_LIMIT),
        name="rwkv_mix",
    )(x, tokens, lw, *consts)


def _conv_kernel(x_ref, gain_ref, win_ref, bin_ref, dw_ref, dwb_ref, lng_ref, lnb_ref, wout_ref, bout_ref,
                 o_ref, ext_ref, sh_ref):
    tm, d = o_ref.shape[1], o_ref.shape[2]
    sub = min(CONV_SUB, tm)
    lanes = min(CONV_LANES, d)
    off = CONV_HALO - (CONV_WIDTH - 1)
    span = sub + CONV_HALO
    nrb = CONV_ROWS // SUBLANES

    @pl.when(pl.program_id(1) == 0)
    def _():
        ext_ref[0:CONV_HALO, :] = jnp.zeros((CONV_HALO, d), F32)

    for st in range(tm // sub):
        r0 = st * sub
        x = x_ref[0, r0:r0 + sub, :]
        h = _rms(x, gain_ref[...]).astype(BF16)
        z = _dot(h, win_ref[...]) + bin_ref[...]
        ext_ref[CONV_HALO + r0:CONV_HALO + r0 + sub, :] = z[:, :d] * jax.nn.sigmoid(z[:, d:])
        window = ext_ref[r0:r0 + span, :]
        for s in range(1, SUBLANES):
            sh_ref[s, r0:r0 + span - SUBLANES, :] = pltpu.roll(window, span - s, axis=0)[:span - SUBLANES]

        blocks = []
        for rb in range(sub // CONV_ROWS):
            lane_blocks = []
            for lb in range(d // lanes):
                ls = slice(lb * lanes, (lb + 1) * lanes)
                acc = jnp.broadcast_to(dwb_ref[:, ls], (CONV_ROWS, lanes)).reshape(nrb, SUBLANES, lanes)
                for j in range(CONV_WIDTH):
                    q, s = divmod(off + j, SUBLANES)
                    lo = r0 + rb * CONV_ROWS + q * SUBLANES
                    win = ext_ref[lo:lo + CONV_ROWS, ls] if s == 0 else sh_ref[s, lo:lo + CONV_ROWS, ls]
                    tap = dw_ref[j * SUBLANES:(j + 1) * SUBLANES, ls]
                    acc = acc + tap[None] * win.reshape(nrb, SUBLANES, lanes)
                lane_blocks.append(acc.reshape(CONV_ROWS, lanes))
            blocks.append(jnp.concatenate(lane_blocks, axis=-1))
        cv = jnp.concatenate(blocks, axis=0)

        mean = jnp.mean(cv, axis=-1, keepdims=True)
        dlt = cv - mean
        var = jnp.mean(dlt * dlt, axis=-1, keepdims=True)
        hn = dlt * lax.rsqrt(var + LN_EPS) * lng_ref[...] + lnb_ref[...]
        act = (hn * jax.nn.sigmoid(hn)).astype(BF16)
        o_ref[0, r0:r0 + sub, :] = x + _dot(act, wout_ref[...]) + bout_ref[...]

    ext_ref[0:CONV_HALO, :] = ext_ref[tm:tm + CONV_HALO, :]


def _conv(x, gain, w_in, b_in, dw8, dw_b, ln_g, ln_b, w_out, b_out, tm):
    b, t, d = x.shape
    tok = pl.BlockSpec((1, tm, d), lambda i, j: (i, j, 0))
    consts = [gain, w_in, b_in, dw8, dw_b, ln_g, ln_b, w_out, b_out]
    return pl.pallas_call(
        _conv_kernel,
        grid=(b, t // tm),
        in_specs=[tok] + [_const_spec(c.shape) for c in consts],
        out_specs=tok,
        out_shape=jax.ShapeDtypeStruct((b, t, d), F32),
        scratch_shapes=[pltpu.VMEM((tm + CONV_HALO, d), F32),
                        pltpu.VMEM((SUBLANES, tm + CONV_HALO - SUBLANES, d), F32)],
        compiler_params=pltpu.CompilerParams(dimension_semantics=("arbitrary", "arbitrary"),
                                             vmem_limit_bytes=VMEM_LIMIT),
        name="conformer_conv",
    )(x, *consts)


def _row(v):
    return v.reshape(1, -1)


def kernel(x, norm_gains, ffn_w_gate, ffn_w_up, ffn_w_down, rwkv_mu, rwkv_w_rkv, rwkv_w0, rwkv_w1, rwkv_w2, rwkv_a0, rwkv_a1, rwkv_a2, rwkv_g1, rwkv_g2, rwkv_k_k, rwkv_k_a, rwkv_r_k, rwkv_ln_gain, rwkv_ln_bias, rwkv_w_out, conv_w_in, conv_b_in, conv_dw, conv_dw_b, conv_ln_gain, conv_ln_bias, conv_w_out, conv_b_out, final_norm):
    bsz, t, d = x.shape
    depth = norm_gains.shape[0]
    m = bsz * t
    tm_ffn = min(1024, m)
    tm_tok = min(256, t)
    tm_proj = min(512, t)
    assert depth >= 1 and d % GROUP_LANES == 0 and m % tm_ffn == 0
    assert t % SCAN_ROWS == 0 and t % tm_tok == 0 and t % tm_proj == 0

    bf = lambda w: w.astype(BF16)
    blk = jnp.arange(GROUP_LANES, dtype=jnp.int32) // HEAD
    ones_bd = (blk[:, None] == blk[None, :]).astype(BF16)
    idx = jnp.arange(SCAN_BLOCK, dtype=jnp.int32)
    tri = ((idx[:, None] >= idx[None, :]) & (idx[:, None] // CHUNK == idx[None, :] // CHUNK)).astype(BF16)

    def ffn(xc, i, s):
        last = i == depth - 1 and s == 1
        out = _ffn(xc.reshape(m, d), _row(norm_gains[i, 2 * s]), bf(ffn_w_gate[i, s]), bf(ffn_w_up[i, s]),
                   bf(ffn_w_down[i, s]), tm_ffn, out_gain=_row(final_norm) if last else None)
        return out.reshape(bsz, t, d)

    for i in range(depth):
        x = ffn(x, i, 0)
        j = i // 2
        gain = _row(norm_gains[i, 1])
        if i % 2 == 0:
            tokens, lw = _rwkv_proj(
                x, gain, bf(rwkv_mu[j]), bf(rwkv_w_rkv[j, 0]), bf(rwkv_w_rkv[j, 1]), bf(rwkv_w_rkv[j, 2]),
                _row(rwkv_w0[j]), bf(rwkv_w1[j]), bf(rwkv_w2[j]), _row(rwkv_a0[j]), bf(rwkv_a1[j]),
                bf(rwkv_a2[j]), bf(rwkv_g1[j]), bf(rwkv_g2[j]), _row(rwkv_k_k[j]), _row(rwkv_k_a[j]),
                ones_bd, tm_proj)
            x = _rwkv_mix(x, tokens, lw, tri, _row(rwkv_r_k[j]), _row(rwkv_ln_gain[j]),
                          _row(rwkv_ln_bias[j]), bf(rwkv_w_out[j]), ones_bd)
        else:
            dw8 = jnp.repeat(conv_dw[j], SUBLANES, axis=0)
            x = _conv(x, gain, bf(conv_w_in[j]), _row(conv_b_in[j]), dw8, _row(conv_dw_b[j]),
                      _row(conv_ln_gain[j]), _row(conv_ln_bias[j]), bf(conv_w_out[j]), _row(conv_b_out[j]),
                      tm_tok)
        x = ffn(x, i, 1)
    return x
```

```python
import jax
import jax.numpy as jnp
from jax import lax
from jax.experimental import pallas as pl
from jax.experimental.pallas import tpu as pltpu

F32 = jnp.float32
BF16 = jnp.bfloat16

HEAD = 64
GROUP_LANES = 256
HEADS_PER_GROUP = GROUP_LANES // HEAD
CHUNK = 64
SCAN_BLOCK = 2 * CHUNK
SCAN_ROWS = 2 * SCAN_BLOCK
FFN_SUB = 256
CONV_WIDTH = 31
CONV_HALO = 32
CONV_SUB = 128
CONV_ROWS = 32
CONV_LANES = 256
SUBLANES = 8
TOKEN_FIELDS = ("r", "k", "v", "a", "b", "g")
RMS_EPS = 1e-6
LN_EPS = 1e-5
GN_EPS = 64e-5
L2_EPS = 1e-12
DECAY_SCALE = 0.6065306597126334
V7X_VMEM_BYTES = 64 * 1024 * 1024
VMEM_LIMIT = V7X_VMEM_BYTES - 8 * 1024 * 1024
FFN_TILE = 1024
PROJ_TILE = 512
CONV_TILE = 512


def _rms(x, gain):
    ms = jnp.mean(x * x, axis=-1, keepdims=True)
    return x * lax.rsqrt(ms + RMS_EPS) * gain


def _dot(a, b):
    return jnp.dot(a, b, preferred_element_type=F32)


def _dot_nt(a, b):
    return lax.dot_general(a, b, (((1,), (1,)), ((), ())), preferred_element_type=F32)


def _dot_tn(a, b):
    return lax.dot_general(a, b, (((0,), (0,)), ((), ())), preferred_element_type=F32)


def _split2(x):
    hi = x.astype(BF16)
    lo = (x - hi.astype(F32)).astype(BF16)
    return hi, lo


def _split3(x):
    hi = x.astype(BF16)
    r1 = x - hi.astype(F32)
    mid = r1.astype(BF16)
    lo = (r1 - mid.astype(F32)).astype(BF16)
    return hi, mid, lo


def _head_sums(xs, ones_bd):
    outs = []
    for x in xs:
        hi, lo = _split2(x)
        slabs = [slice(s * GROUP_LANES, (s + 1) * GROUP_LANES) for s in range(x.shape[-1] // GROUP_LANES)]
        outs.append(jnp.concatenate([_dot(hi[:, sl], ones_bd) + _dot(lo[:, sl], ones_bd) for sl in slabs],
                                    axis=-1))
    return outs


def _const_spec(shape):
    nd = len(shape)
    return pl.BlockSpec(shape, lambda *_: (0,) * nd, pipeline_mode=pl.Buffered(1))


def _ffn_kernel(x_ref, g_ref, wg_ref, wu_ref, wd_ref, *rest):
    out_gain_ref, o_ref = rest if len(rest) == 2 else (None, rest[0])
    tm = o_ref.shape[0]
    sub = min(FFN_SUB, tm)
    rows = lambda s: slice(s * sub, (s + 1) * sub)
    normed = lambda s: _rms(x_ref[rows(s), :], g_ref[...]).astype(BF16)

    def finish(s, down):
        y = x_ref[rows(s), :] + 0.5 * down
        o_ref[rows(s), :] = y if out_gain_ref is None else _rms(y, out_gain_ref[...])

    h_next = normed(0)
    finish_prev = None
    for s in range(tm // sub):
        h = h_next
        if (s + 1) * sub < tm:
            h_next = normed(s + 1)
        gate = _dot(h, wg_ref[...])
        up = _dot(h, wu_ref[...])
        if finish_prev is not None:
            finish_prev()
        act = (gate * jax.nn.sigmoid(gate) * up).astype(BF16)
        down = _dot(act, wd_ref[...])
        finish_prev = lambda s=s, down=down: finish(s, down)
    finish_prev()


def _ffn(x2, gain, wg, wu, wd, tm, out_gain=None):
    m, d = x2.shape
    consts = [gain, wg, wu, wd] + ([] if out_gain is None else [out_gain])
    return pl.pallas_call(
        _ffn_kernel,
        grid=(m // tm,),
        in_specs=[pl.BlockSpec((tm, d), lambda i: (i, 0))] + [_const_spec(c.shape) for c in consts],
        out_specs=pl.BlockSpec((tm, d), lambda i: (i, 0)),
        out_shape=jax.ShapeDtypeStruct((m, d), F32),
        compiler_params=pltpu.CompilerParams(dimension_semantics=("arbitrary",),
                                             vmem_limit_bytes=VMEM_LIMIT),
        name="ffn",
    )(x2, *consts)


def _rwkv_proj_kernel(x_ref, gain_ref, mu_ref, wr_ref, wk_ref, wv_ref, w0_ref, w1_ref, w2_ref,
                      a0_ref, a1_ref, a2_ref, g1_ref, g2_ref, kk_ref, ka_ref, ones_ref,
                      tok_out, lw_out, carry_ref):
    t = pl.program_id(1)

    @pl.when(t == 0)
    def _():
        carry_ref[...] = jnp.zeros_like(carry_ref)

    h = _rms(x_ref[0], gain_ref[...])
    tm = h.shape[0]
    row = lax.broadcasted_iota(jnp.int32, h.shape, 0)
    h_prev = jnp.where(row == 0, carry_ref[0:1, :], pltpu.roll(h, 1, axis=0))
    carry_ref[0:1, :] = h[tm - 1:tm, :]
    hb = h.astype(BF16)
    xxb = (h_prev - h).astype(BF16)
    mix = lambda i: hb + xxb * mu_ref[i:i + 1, :]

    r = _dot(mix(0), wr_ref[...])
    k = _dot(mix(2), wk_ref[...])
    v = _dot(mix(3), wv_ref[...])
    zw = w0_ref[...] + _dot(jnp.tanh(_dot(mix(1), w1_ref[...])).astype(BF16), w2_ref[...])
    lw = -DECAY_SCALE * jax.nn.sigmoid(zw)
    a = jax.nn.sigmoid(a0_ref[...] + _dot(_dot(mix(4), a1_ref[...]).astype(BF16), a2_ref[...]))
    g = _dot(jax.nn.sigmoid(_dot(mix(5), g1_ref[...])).astype(BF16), g2_ref[...])

    kk = k * kk_ref[...]
    norm = jnp.sqrt(_head_sums([kk * kk], ones_ref[...])[0])
    kk = kk / jnp.maximum(norm, L2_EPS)

    lw_out[0] = lw
    d = lw.shape[-1]
    fields = dict(r=r, k=k * (1.0 + (a - 1.0) * ka_ref[...]), v=v, a=-kk, b=kk * a, g=g)
    for i, name in enumerate(TOKEN_FIELDS):
        tok_out[0, :, i * d:(i + 1) * d] = fields[name].astype(BF16)


def _rwkv_proj(x, gain, mu, wr, wk, wv, w0, w1, w2, a0, a1, a2, g1, g2, k_k, k_a, ones_bd, tm):
    b, t, d = x.shape
    tok = pl.BlockSpec((1, tm, d), lambda i, j: (i, j, 0))
    consts = [gain, mu, wr, wk, wv, w0, w1, w2, a0, a1, a2, g1, g2, k_k, k_a, ones_bd]
    nf = len(TOKEN_FIELDS)
    return pl.pallas_call(
        _rwkv_proj_kernel,
        grid=(b, t // tm),
        in_specs=[tok] + [_const_spec(c.shape) for c in consts],
        out_specs=[pl.BlockSpec((1, tm, nf * d), lambda i, j: (i, j, 0)), tok],
        out_shape=[jax.ShapeDtypeStruct((b, t, nf * d), BF16), jax.ShapeDtypeStruct((b, t, d), F32)],
        scratch_shapes=[pltpu.VMEM((SUBLANES, d), F32)],
        compiler_params=pltpu.CompilerParams(dimension_semantics=("arbitrary", "arbitrary"),
                                             vmem_limit_bytes=VMEM_LIMIT),
        name="rwkv_proj",
    )(x, *consts)


def _stack_heads(x, blk):
    return jnp.concatenate(
        [jnp.where(blk == h, x, 0.0).astype(BF16) for h in range(HEADS_PER_GROUP)], axis=0)


_SCALED_NAMES = ("r_t", "a_t", "k_t", "b_t", "k_h", "b_h", "g_end", "vv")


def _scan_scale_tasks(load, tri, scaled):
    c, gl = CHUNK, GROUP_LANES
    rows = tri.shape[0]
    for name in _SCALED_NAMES:
        scaled[name] = []
    cs_box = []

    def cumulative_decay():
        hi, mid, lo = _split3(load("lw", slice(0, rows)))
        cs_box.append(_dot(tri, hi) + _dot(tri, mid) + _dot(tri, lo))

    def scale_chunk(ci):
        rs = slice(ci * c, (ci + 1) * c)
        cs_c = cs_box[0][rs]
        cs_end = cs_c[c - 1:c, :]
        e_neg = jnp.exp(-cs_c)
        e_end = jnp.exp(cs_end - cs_c)
        k, b = load("k", rs), load("b", rs)
        full = dict(r_t=load("r", rs) * jnp.exp(cs_c), a_t=load("a", rs) * jnp.exp(cs_c - load("lw", rs)),
                    k_t=k * e_neg, b_t=b * e_neg, k_h=k * e_end, b_h=b * e_end,
                    g_end=jnp.exp(cs_end), vv=load("v", rs))
        for g in range(cs_c.shape[1] // gl):
            for name in _SCALED_NAMES:
                scaled[name].append(full[name][:, g * gl:(g + 1) * gl])

    return [cumulative_decay] + [lambda ci=ci: scale_chunk(ci) for ci in range(rows // c)]


def _scan_local(scaled, hook):
    r_t, a_t, k_t, b_t, k_h, b_h, g_end, vv = (scaled[name] for name in _SCALED_NAMES)
    c, gl = CHUNK, GROUP_LANES
    assert HEADS_PER_GROUP * c == gl
    nu = range(len(r_t))

    rowi = lax.broadcasted_iota(jnp.int32, (c, gl), 0)
    lane = lax.broadcasted_iota(jnp.int32, (c, gl), 1)
    blk = lane // HEAD
    col = lane - blk * HEAD
    strict = col < rowi
    incl = col <= rowi
    eye = jnp.where(col == rowi, 1.0, 0.0)
    stack = lambda x: _stack_heads(x, blk)

    aa = [_dot_nt(jnp.concatenate([a_t[u], r_t[u]], axis=0).astype(BF16),
                  jnp.concatenate([stack(b_t[u]), stack(k_t[u])], axis=0)) for u in nu]
    p = [jnp.where(strict, aa[u][:c, :gl], 0.0) for u in nu]
    a_xk = [jnp.concatenate([jnp.where(strict, aa[u][:c, gl:], 0.0),
                             jnp.where(incl, aa[u][c:, gl:], 0.0)], axis=0).astype(BF16) for u in nu]
    a_rb = [jnp.where(incl, aa[u][c:, :gl], 0.0) for u in nu]
    hook()

    xkv = [_dot(a_xk[u], stack(vv[u])) for u in nu]
    hook()

    n_sq = c.bit_length() - 1
    assert n_sq >= 2
    both = [_dot(jnp.concatenate([p[u], a_rb[u]], axis=0).astype(BF16), stack(p[u])) for u in nu]
    tq = [jnp.concatenate([eye + p[u], a_rb[u] + both[u][c:]], axis=0) for u in nu]
    p = [both[u][:c] for u in nu]
    hook()
    for i in range(1, n_sq):
        if i < n_sq - 1:
            both = [_dot(jnp.concatenate([p[u], tq[u]], axis=0).astype(BF16), stack(p[u])) for u in nu]
            p = [both[u][:c] for u in nu]
            tq = [tq[u] + both[u][c:] for u in nu]
        else:
            tq = [tq[u] + _dot(tq[u].astype(BF16), stack(p[u])) for u in nu]
        hook()

    tx = [_dot(tq[u].astype(BF16), jnp.concatenate([stack(a_t[u]), stack(xkv[u][:c])], axis=1)) for u in nu]
    hook()
    return dict(
        ra_hat=[jnp.concatenate([r_t[u] + tx[u][c:, :gl], tx[u][:c, :gl]], axis=0).astype(BF16) for u in nu],
        y0=[tx[u][c:, gl:] + xkv[u][c:] for u in nu],
        u0=[tx[u][:c, gl:] for u in nu],
        bk_h=[jnp.concatenate([b_h[u], k_h[u]], axis=0).astype(BF16) for u in nu],
        vv=vv, g_end=g_end)


def _scan_state_tasks(loc, states, ys):
    c, gl = CHUNK, GROUP_LANES
    n_groups = len(states)
    srow = lax.broadcasted_iota(jnp.int32, (gl, gl), 0) // HEAD
    scol = lax.broadcasted_iota(jnp.int32, (gl, gl), 1) // HEAD
    on_diag = srow == scol
    tasks = []
    for ci in range(len(loc["y0"]) // n_groups):
        us = [ci * n_groups + g for g in range(n_groups)]
        nt = []

        def read_state(us=us, nt=nt):
            nt[:] = [_dot_nt(loc["ra_hat"][u], states[g].astype(BF16)) for g, u in enumerate(us)]
            ys.append(jnp.concatenate([nt[g][:c] + loc["y0"][u] for g, u in enumerate(us)], axis=-1))

        def advance_state(us=us, nt=nt):
            uv = [jnp.concatenate([nt[g][c:] + loc["u0"][u], loc["vv"][u]], axis=0).astype(BF16)
                  for g, u in enumerate(us)]
            states[:] = [jnp.where(on_diag, states[g] * loc["g_end"][u] + _dot_tn(uv[g], loc["bk_h"][u]), 0.0)
                         for g, u in enumerate(us)]

        tasks += [read_state, advance_state]
    return tasks


def _rwkv_mix_kernel(x_ref, tok_ref, lw_ref, tri_ref, rk_ref, lng_ref, lnb_ref, wo_ref, ones_ref, o_ref, s_ref):
    @pl.when(pl.program_id(1) == 0)
    def _():
        s_ref[...] = jnp.zeros_like(s_ref)

    n_groups = s_ref.shape[0]
    n_blocks = o_ref.shape[1] // SCAN_BLOCK
    d = o_ref.shape[2]

    def token_rows(name, rs):
        if name == "lw":
            return lw_ref[0, rs, :]
        i = TOKEN_FIELDS.index(name)
        return tok_ref[0, rs, i * d:(i + 1) * d].astype(F32)

    states = [s_ref[g] for g in range(n_groups)]
    pending = []

    def hook():
        if pending:
            pending.pop(0)()

    def block_loader(blk_i):
        base = blk_i * SCAN_BLOCK
        return lambda name, rs: token_rows(name, slice(base + rs.start, base + rs.stop))

    scaled = [dict() for _ in range(n_blocks)]
    scale_tasks = [_scan_scale_tasks(block_loader(i), tri_ref[...], scaled[i]) for i in range(n_blocks)]
    ys = [[] for _ in range(n_blocks)]
    for task in scale_tasks[0]:
        task()
    for blk_i in range(n_blocks):
        next_scale = scale_tasks[blk_i + 1] if blk_i + 1 < n_blocks else []
        pending += next_scale
        loc = _scan_local(scaled[blk_i], hook)
        while any(task in pending for task in next_scale):
            pending.pop(0)()
        pending += _scan_state_tasks(loc, states, ys[blk_i])

    ones_bd = ones_ref[...]
    for blk_i in range(n_blocks):
        rs = slice(blk_i * SCAN_BLOCK, (blk_i + 1) * SCAN_BLOCK)
        while len(ys[blk_i]) < SCAN_BLOCK // CHUNK:
            pending.pop(0)()
        y = jnp.concatenate(ys[blk_i], axis=0)
        r, k, v = token_rows("r", rs), token_rows("k", rs), token_rows("v", rs)
        y_sum, rk_sum = _head_sums([y, r * k * rk_ref[...]], ones_bd)
        hook()
        dlt = y - y_sum * (1.0 / HEAD)
        var = _head_sums([dlt * dlt], ones_bd)[0] * (1.0 / HEAD)
        hook()
        yn = dlt * lax.rsqrt(var + GN_EPS) * lng_ref[...] + lnb_ref[...]
        out = ((yn + rk_sum * v) * token_rows("g", rs)).astype(BF16)
        hook()
        o_ref[0, rs, :] = x_ref[0, rs, :] + _dot(out, wo_ref[...])
        hook()
    while pending:
        pending.pop(0)()
    for g in range(n_groups):
        s_ref[g] = states[g]


def _rwkv_mix(x, tokens, lw, tri, r_k, ln_g, ln_b, wo, ones_bd):
    bsz, t, d = x.shape
    tok = pl.BlockSpec((1, SCAN_ROWS, d), lambda i, j: (i, j, 0))
    wide = pl.BlockSpec((1, SCAN_ROWS, tokens.shape[2]), lambda i, j: (i, j, 0))
    consts = [tri, r_k, ln_g, ln_b, wo, ones_bd]
    return pl.pallas_call(
        _rwkv_mix_kernel,
        grid=(bsz, t // SCAN_ROWS),
        in_specs=[tok, wide, tok] + [_const_spec(c.shape) for c in consts],
        out_specs=tok,
        out_shape=jax.ShapeDtypeStruct((bsz, t, d), F32),
        scratch_shapes=[pltpu.VMEM((d // GROUP_LANES, GROUP_LANES, GROUP_LANES), F32)],
        compiler_params=pltpu.CompilerParams(dimension_semantics=("arbitrary", "arbitrary"),
                                             vmem_limit_bytes=VMEM_LIMIT),
        name="rwkv_mix",
    )(x, tokens, lw, *consts)


def _conv_kernel(x_ref, gain_ref, win_ref, bin_ref, dw_ref, dwb_ref, lng_ref, lnb_ref, wout_ref, bout_ref,
                 o_ref, ext_ref, sh_ref):
    tm, d = o_ref.shape[1], o_ref.shape[2]
    sub = min(CONV_SUB, tm)
    lanes = min(CONV_LANES, d)
    off = CONV_HALO - (CONV_WIDTH - 1)
    span = sub + CONV_HALO
    nrb = CONV_ROWS // SUBLANES

    @pl.when(pl.program_id(1) == 0)
    def _():
        ext_ref[0:CONV_HALO, :] = jnp.zeros((CONV_HALO, d), F32)

    for st in range(tm // sub):
        r0 = st * sub
        x = x_ref[0, r0:r0 + sub, :]
        h = _rms(x, gain_ref[...]).astype(BF16)
        z = _dot(h, win_ref[...]) + bin_ref[...]
        ext_ref[CONV_HALO + r0:CONV_HALO + r0 + sub, :] = z[:, :d] * jax.nn.sigmoid(z[:, d:])
        window = ext_ref[r0:r0 + span, :]
        for s in range(1, SUBLANES):
            sh_ref[s, r0:r0 + span - SUBLANES, :] = pltpu.roll(window, span - s, axis=0)[:span - SUBLANES]

        blocks = []
        for rb in range(sub // CONV_ROWS):
            lane_blocks = []
            for lb in range(d // lanes):
                ls = slice(lb * lanes, (lb + 1) * lanes)
                acc = jnp.broadcast_to(dwb_ref[:, ls], (CONV_ROWS, lanes)).reshape(nrb, SUBLANES, lanes)
                for j in range(CONV_WIDTH):
                    q, s = divmod(off + j, SUBLANES)
                    lo = r0 + rb * CONV_ROWS + q * SUBLANES
                    win = ext_ref[lo:lo + CONV_ROWS, ls] if s == 0 else sh_ref[s, lo:lo + CONV_ROWS, ls]
                    tap = dw_ref[j * SUBLANES:(j + 1) * SUBLANES, ls]
                    acc = acc + tap[None] * win.reshape(nrb, SUBLANES, lanes)
                lane_blocks.append(acc.reshape(CONV_ROWS, lanes))
            blocks.append(jnp.concatenate(lane_blocks, axis=-1))
        cv = jnp.concatenate(blocks, axis=0)

        mean = jnp.mean(cv, axis=-1, keepdims=True)
        dlt = cv - mean
        var = jnp.mean(dlt * dlt, axis=-1, keepdims=True)
        hn = dlt * lax.rsqrt(var + LN_EPS) * lng_ref[...] + lnb_ref[...]
        act = (hn * jax.nn.sigmoid(hn)).astype(BF16)
        o_ref[0, r0:r0 + sub, :] = x + _dot(act, wout_ref[...]) + bout_ref[...]

    ext_ref[0:CONV_HALO, :] = ext_ref[tm:tm + CONV_HALO, :]


def _conv(x, gain, w_in, b_in, dw8, dw_b, ln_g, ln_b, w_out, b_out, tm):
    b, t, d = x.shape
    tok = pl.BlockSpec((1, tm, d), lambda i, j: (i, j, 0))
    consts = [gain, w_in, b_in, dw8, dw_b, ln_g, ln_b, w_out, b_out]
    return pl.pallas_call(
        _conv_kernel,
        grid=(b, t // tm),
        in_specs=[tok] + [_const_spec(c.shape) for c in consts],
        out_specs=tok,
        out_shape=jax.ShapeDtypeStruct((b, t, d), F32),
        scratch_shapes=[pltpu.VMEM((tm + CONV_HALO, d), F32),
                        pltpu.VMEM((SUBLANES, tm + CONV_HALO - SUBLANES, d), F32)],
        compiler_params=pltpu.CompilerParams(dimension_semantics=("arbitrary", "arbitrary"),
                                             vmem_limit_bytes=VMEM_LIMIT),
        name="conformer_conv",
    )(x, *consts)


def _row(v):
    return v.reshape(1, -1)


def kernel(x, norm_gains, ffn_w_gate, ffn_w_up, ffn_w_down, rwkv_mu, rwkv_w_rkv, rwkv_w0, rwkv_w1, rwkv_w2, rwkv_a0, rwkv_a1, rwkv_a2, rwkv_g1, rwkv_g2, rwkv_k_k, rwkv_k_a, rwkv_r_k, rwkv_ln_gain, rwkv_ln_bias, rwkv_w_out, conv_w_in, conv_b_in, conv_dw, conv_dw_b, conv_ln_gain, conv_ln_bias, conv_w_out, conv_b_out, final_norm):
    bsz, t, d = x.shape
    depth = norm_gains.shape[0]
    m = bsz * t
    tm_ffn, tm_tok, tm_proj = min(FFN_TILE, m), min(CONV_TILE, t), min(PROJ_TILE, t)
    assert depth >= 1 and d % GROUP_LANES == 0 and m % tm_ffn == 0
    assert t % SCAN_ROWS == 0 and t % tm_tok == 0 and t % tm_proj == 0

    bf = lambda w: w.astype(BF16)
    blk = jnp.arange(GROUP_LANES, dtype=jnp.int32) // HEAD
    ones_bd = (blk[:, None] == blk[None, :]).astype(BF16)
    idx = jnp.arange(SCAN_BLOCK, dtype=jnp.int32)
    tri = ((idx[:, None] >= idx[None, :]) & (idx[:, None] // CHUNK == idx[None, :] // CHUNK)).astype(BF16)

    def ffn(xc, i, s):
        last = i == depth - 1 and s == 1
        out = _ffn(xc.reshape(m, d), _row(norm_gains[i, 2 * s]), bf(ffn_w_gate[i, s]), bf(ffn_w_up[i, s]),
                   bf(ffn_w_down[i, s]), tm_ffn, out_gain=_row(final_norm) if last else None)
        return out.reshape(bsz, t, d)

    for i in range(depth):
        x = ffn(x, i, 0)
        j = i // 2
        gain = _row(norm_gains[i, 1])
        if i % 2 == 0:
            tokens, lw = _rwkv_proj(
                x, gain, bf(rwkv_mu[j]), bf(rwkv_w_rkv[j, 0]), bf(rwkv_w_rkv[j, 1]), bf(rwkv_w_rkv[j, 2]),
                _row(rwkv_w0[j]), bf(rwkv_w1[j]), bf(rwkv_w2[j]), _row(rwkv_a0[j]), bf(rwkv_a1[j]),
                bf(rwkv_a2[j]), bf(rwkv_g1[j]), bf(rwkv_g2[j]), _row(rwkv_k_k[j]), _row(rwkv_k_a[j]),
                ones_bd, tm_proj)
            x = _rwkv_mix(x, tokens, lw, tri, _row(rwkv_r_k[j]), _row(rwkv_ln_gain[j]),
                          _row(rwkv_ln_bias[j]), bf(rwkv_w_out[j]), ones_bd)
        else:
            dw8 = jnp.repeat(conv_dw[j], SUBLANES, axis=0)
            x = _conv(x, gain, bf(conv_w_in[j]), _row(conv_b_in[j]), dw8, _row(conv_dw_b[j]),
                      _row(conv_ln_gain[j]), _row(conv_ln_bias[j]), bf(conv_w_out[j]), _row(conv_b_out[j]),
                      tm_tok)
        x = ffn(x, i, 1)
    return x
```
